```python
import math
import jax, jax.numpy as jnp
from jax import lax
import numpy as np

D_MODEL = 2048
BATCH = 4
SEQ = 8192
DEPTH = 2

CHUNK = 64
N_META = 16
MIX_WIDTH = D_MODEL
HEAD_DIM = 64
ATTN_WIDTH = MIX_WIDTH // 4
ATTN_HEADS = ATTN_WIDTH // HEAD_DIM
SSM_WIDTH = MIX_WIDTH // 4
SSM_GROUP = 16
SSM_GROUPS = SSM_WIDTH // SSM_GROUP
SSM_STATE = 64
CONV_WIDTH = MIX_WIDTH - ATTN_WIDTH - SSM_WIDTH
DW_CONV_SIZE = 31
IN_WIDTH = 3 * ATTN_WIDTH + SSM_WIDTH + 2 * CONV_WIDTH
D_FF = 5632
Q_BLOCK = 128
RMS_EPS = 1e-6
LN_EPS = 1e-5
DT_MIN = 1e-3
DT_MAX = 1e-1
LAMBDA_RE_MAX = -1e-4

kernel_name = 'hybrid_stickbreak_s5_conformer_trunk'


def rms_norm(x, g):
    x32 = x.astype(jnp.float32)
    y = x32 * lax.rsqrt(jnp.mean(x32 * x32, axis=-1, keepdims=True) + RMS_EPS)
    return (y * g.astype(jnp.float32)).astype(x.dtype)


def layer_norm(x, g, b):
    x32 = x.astype(jnp.float32)
    xc = x32 - jnp.mean(x32, axis=-1, keepdims=True)
    y = xc * lax.rsqrt(jnp.mean(xc * xc, axis=-1, keepdims=True) + LN_EPS)
    return (y * g.astype(jnp.float32) + b.astype(jnp.float32)).astype(x.dtype)


def swiglu_ffn(x, w_gate, w_up, w_down):
    return (jax.nn.silu(x @ w_gate) * (x @ w_up)) @ w_down


def stick_breaking_attention(q, k, v):
    b, lp, h, dh = q.shape
    nb = lp // Q_BLOCK
    scale = dh ** -0.5
    k32 = k.astype(jnp.float32)
    v32 = v.astype(jnp.float32)
    kpos = jnp.arange(lp)
    q_blocks = q.reshape(b, nb, Q_BLOCK, h, dh).swapaxes(0, 1)

    def one_block(args):
        q_blk, blk = args
        qpos = blk * Q_BLOCK + jnp.arange(Q_BLOCK)
        earlier = kpos[None, :] < qpos[:, None]
        z = jnp.einsum('bqhd,bkhd->bhqk', q_blk.astype(jnp.float32), k32) * scale
        log_keep = jnp.where(earlier, jax.nn.log_sigmoid(-z), 0.0)
        log_survive = lax.cumsum(log_keep, axis=3, reverse=True) - log_keep
        att = jnp.where(earlier, jnp.exp(jax.nn.log_sigmoid(z) + log_survive), 0.0)
        return jnp.einsum('bhqk,bkhd->bqhd', att, v32)

    out = lax.map(one_block, (q_blocks, jnp.arange(nb)))
    return out.swapaxes(0, 1).reshape(b, lp, h, dh).astype(q.dtype)


def s5_ssm(u, lam_re, lam_im, log_step, b_re, b_im, c_re, c_im, d_skip):
    f32 = jnp.float32
    lam = lax.complex(jnp.minimum(lam_re.astype(f32), LAMBDA_RE_MAX), lam_im.astype(f32))
    step = jnp.exp(log_step.astype(f32))[:, None]
    lam_bar = jnp.exp(lam * step)
    b_cplx = lax.complex(b_re.astype(f32), b_im.astype(f32))
    b_bar = ((lam_bar - 1.0) / lam)[:, :, None] * b_cplx
    c_cplx = lax.complex(c_re.astype(f32), c_im.astype(f32))
    u32 = u.astype(f32)
    bu = jnp.einsum('gph,blgh->blgp', b_bar, u32.astype(jnp.complex64))
    decay = jnp.broadcast_to(lam_bar, bu.shape)

    def combine(left, right):
        a_l, s_l = left
        a_r, s_r = right
        return a_l * a_r, a_r * s_l + s_r

    _, states = lax.associative_scan(combine, (decay, bu), axis=1)
    return jnp.real(jnp.einsum('ghp,blgp->blgh', c_cplx, states)) + d_skip.astype(f32) * u32


def conformer_conv(a, gate, w_dw, b_dw, ln_g, ln_b, w_pw, b_pw):
    hidden = a * jax.nn.sigmoid(gate)
    hidden = jnp.pad(hidden, ((0, 0), (DW_CONV_SIZE - 1, 0), (0, 0)))
    hidden = lax.conv_general_dilated(
        hidden, w_dw[:, None, :], window_strides=(1,), padding='VALID',
        dimension_numbers=('NWC', 'WIO', 'NWC'), feature_group_count=CONV_WIDTH) + b_dw
    hidden = jax.nn.silu(layer_norm(hidden, ln_g, ln_b))
    return hidden @ w_pw + b_pw


def setup_inputs(seed: int = 0) -> dict:
    key = jax.random.key(seed)
    ks = iter(jax.random.split(key, 40))
    f32 = jnp.float32

    def nrm(shape, scale):
        return scale * jax.random.normal(next(ks), shape, f32)

    def gain(shape):
        return 1.0 + 0.02 * jax.random.normal(next(ks), shape, f32)

    lam_im_base = jnp.pi * jnp.arange(SSM_STATE, dtype=f32)
    return {
        'x': nrm((BATCH, SEQ, D_MODEL), 1.0),
        'meta': nrm((N_META, D_MODEL), 1.0),
        'ffn1_norm': gain((DEPTH, D_MODEL)),
        'ffn1_w_gate': nrm((DEPTH, D_MODEL, D_FF), D_MODEL ** -0.5),
        'ffn1_w_up': nrm((DEPTH, D_MODEL, D_FF), D_MODEL ** -0.5),
        'ffn1_w_down': nrm((DEPTH, D_FF, D_MODEL), D_FF ** -0.5),
        'mix_norm': gain((DEPTH, D_MODEL)),
        'w_in': nrm((DEPTH, D_MODEL, IN_WIDTH), D_MODEL ** -0.5),
        'q_norm': gain((DEPTH, HEAD_DIM)),
        'k_norm': gain((DEPTH, HEAD_DIM)),
        'ssm_lambda_re': -0.5 + nrm((DEPTH, SSM_GROUPS, SSM_STATE), 0.01),
        'ssm_lambda_im': lam_im_base + nrm((DEPTH, SSM_GROUPS, SSM_STATE), 0.01),
        'ssm_log_step': jax.random.uniform(next(ks), (DEPTH, SSM_GROUPS), f32,
                                           math.log(DT_MIN), math.log(DT_MAX)),
        'ssm_b_re': nrm((DEPTH, SSM_GROUPS, SSM_STATE, SSM_GROUP), (2 * SSM_GROUP) ** -0.5),
        'ssm_b_im': nrm((DEPTH, SSM_GROUPS, SSM_STATE, SSM_GROUP), (2 * SSM_GROUP) ** -0.5),
        'ssm_c_re': nrm((DEPTH, SSM_GROUPS, SSM_GROUP, SSM_STATE), (2 * SSM_STATE) ** -0.5),
        'ssm_c_im': nrm((DEPTH, SSM_GROUPS, SSM_GROUP, SSM_STATE), (2 * SSM_STATE) ** -0.5),
        'ssm_d': nrm((DEPTH, SSM_GROUPS, SSM_GROUP), 1.0),
        'ssm_w_glu': nrm((DEPTH, SSM_WIDTH, SSM_WIDTH), SSM_WIDTH ** -0.5),
        'ssm_b_glu': nrm((DEPTH, SSM_WIDTH), 0.02),
        'conv_w_dw': nrm((DEPTH, DW_CONV_SIZE, CONV_WIDTH), DW_CONV_SIZE ** -0.5),
        'conv_b_dw': nrm((DEPTH, CONV_WIDTH), 0.02),
        'conv_ln_g': gain((DEPTH, CONV_WIDTH)),
        'conv_ln_b': nrm((DEPTH, CONV_WIDTH), 0.02),
        'conv_w_pw': nrm((DEPTH, CONV_WIDTH, CONV_WIDTH), CONV_WIDTH ** -0.5),
        'conv_b_pw': nrm((DEPTH, CONV_WIDTH), 0.02),
        'w_out': nrm((DEPTH, MIX_WIDTH, D_MODEL), MIX_WIDTH ** -0.5),
        'ffn2_norm': gain((DEPTH, D_MODEL)),
        'ffn2_w_gate': nrm((DEPTH, D_MODEL, D_FF), D_MODEL ** -0.5),
        'ffn2_w_up': nrm((DEPTH, D_MODEL, D_FF), D_MODEL ** -0.5),
        'ffn2_w_down': nrm((DEPTH, D_FF, D_MODEL), D_FF ** -0.5),
        'post_norm': gain((DEPTH, D_MODEL)),
    }


def reference(x, meta, ffn1_norm, ffn1_w_gate, ffn1_w_up, ffn1_w_down, mix_norm, w_in,
              q_norm, k_norm, ssm_lambda_re, ssm_lambda_im, ssm_log_step, ssm_b_re, ssm_b_im,
              ssm_c_re, ssm_c_im, ssm_d, ssm_w_glu, ssm_b_glu, conv_w_dw, conv_b_dw,
              conv_ln_g, conv_ln_b, conv_w_pw, conv_b_pw, w_out, ffn2_norm, ffn2_w_gate,
              ffn2_w_up, ffn2_w_down, post_norm):
    batch = x.shape[0]
    meta_b = jnp.broadcast_to(meta.astype(x.dtype)[None], (batch, N_META, D_MODEL))
    h = jnp.concatenate([meta_b, x], axis=1)
    length = h.shape[1]
    pad = (-length) % Q_BLOCK
    split_at = [ATTN_WIDTH, 2 * ATTN_WIDTH, 3 * ATTN_WIDTH,
                3 * ATTN_WIDTH + SSM_WIDTH, 3 * ATTN_WIDTH + SSM_WIDTH + CONV_WIDTH]
    seq_pad = ((0, 0), (0, pad), (0, 0), (0, 0))

    for l in range(DEPTH):
        h = h + 0.5 * swiglu_ffn(rms_norm(h, ffn1_norm[l]), ffn1_w_gate[l], ffn1_w_up[l], ffn1_w_down[l])

        proj = rms_norm(h, mix_norm[l]) @ w_in[l]
        q, k, v, u, conv_a, conv_g = jnp.split(proj, split_at, axis=-1)

        q = rms_norm(q.reshape(batch, length, ATTN_HEADS, HEAD_DIM), q_norm[l])
        k = rms_norm(k.reshape(batch, length, ATTN_HEADS, HEAD_DIM), k_norm[l])
        v = v.reshape(batch, length, ATTN_HEADS, HEAD_DIM)
        attn = stick_breaking_attention(jnp.pad(q, seq_pad), jnp.pad(k, seq_pad), jnp.pad(v, seq_pad))
        attn = attn[:, :length].reshape(batch, length, ATTN_WIDTH)

        y = s5_ssm(u.reshape(batch, length, SSM_GROUPS, SSM_GROUP),
                   ssm_lambda_re[l], ssm_lambda_im[l], ssm_log_step[l], ssm_b_re[l], ssm_b_im[l],
                   ssm_c_re[l], ssm_c_im[l], ssm_d[l])
        y = jax.nn.gelu(y.reshape(batch, length, SSM_WIDTH).astype(h.dtype))
        ssm_out = y * jax.nn.sigmoid(y @ ssm_w_glu[l] + ssm_b_glu[l])

        conv_out = conformer_conv(conv_a, conv_g, conv_w_dw[l], conv_b_dw[l],
                                  conv_ln_g[l], conv_ln_b[l], conv_w_pw[l], conv_b_pw[l])

        mixed = jnp.concatenate([attn, ssm_out, conv_out], axis=-1) @ w_out[l]
        h = h + mixed

        h = h + 0.5 * swiglu_ffn(rms_norm(h, ffn2_norm[l]), ffn2_w_gate[l], ffn2_w_up[l], ffn2_w_down[l])
        h = rms_norm(h, post_norm[l])

    return h[:, N_META:]
```

```python
import functools

import jax
import jax.numpy as jnp
from jax import lax
from jax.experimental import pallas as pl
from jax.experimental.pallas import tpu as pltpu

_F32 = jnp.float32
_BF16 = jnp.bfloat16

HEAD_DIM = 64
SSM_GROUP = 16
SSM_CHUNK = 16
SSM_GROUPS_PER_STEP = 2
DW_TAPS_HALO = 32
RMS_EPS = 1e-6
LN_EPS = 1e-5
LAMBDA_RE_MAX = -1e-4
LANES = 128
SUBLANES = 8
ATT_BLOCK = 128
VMEM_LIMIT_BYTES = 56 * 1024 * 1024


def _pick_tile(n, target, mult):
    best = None
    for t in range(mult, min(n, target) + 1, mult):
        if n % t == 0:
            best = t
    assert best is not None, (n, target, mult)
    return best


def _params(*sem):
    return pltpu.CompilerParams(dimension_semantics=sem,
                                vmem_limit_bytes=VMEM_LIMIT_BYTES)


def _rms(x, g):
    ms = jnp.mean(x * x, axis=-1, keepdims=True)
    return x * lax.rsqrt(ms + RMS_EPS) * g


def _resident(shape):
    zeros = (0,) * len(shape)
    return pl.BlockSpec(shape, lambda *_: zeros, pipeline_mode=pl.Buffered(1))


def _ffn_kernel(*refs, post, nf):
    if post:
        h_ref, g_ref, wg_ref, wu_ref, wd_ref, pg_ref, o_ref, xn_ref = refs
    else:
        h_ref, g_ref, wg_ref, wu_ref, wd_ref, o_ref, xn_ref = refs
    f = pl.program_id(1)

    @pl.when(f == 0)
    def _():
        x = h_ref[...]
        xn_ref[...] = _rms(x, g_ref[...]).astype(_BF16)
        o_ref[...] = x

    xn = xn_ref[...]
    gate = jnp.dot(xn, wg_ref[...], preferred_element_type=_F32)
    up = jnp.dot(xn, wu_ref[...], preferred_element_type=_F32)
    act = (gate * jax.nn.sigmoid(gate)) * (up * 0.5)
    o_ref[...] += jnp.dot(act.astype(_BF16), wd_ref[...],
                          preferred_element_type=_F32)

    if post:
        @pl.when(f == nf - 1)
        def _():
            o_ref[...] = _rms(o_ref[...], pg_ref[...])


def _ffn(h, g, wg, wu, wd, post_g=None):
    n, d = h.shape
    dff = wg.shape[1]
    tm = _pick_tile(n, 640, SUBLANES)
    tf = _pick_tile(dff, 512, LANES)
    post = post_g is not None
    row = pl.BlockSpec((tm, d), lambda i, j: (i, 0))
    vec = pl.BlockSpec((1, d), lambda i, j: (0, 0))
    in_specs = [row, vec,
                pl.BlockSpec((d, tf), lambda i, j: (0, j)),
                pl.BlockSpec((d, tf), lambda i, j: (0, j)),
                pl.BlockSpec((tf, d), lambda i, j: (j, 0))]
    args = [h, g, wg, wu, wd]
    if post:
        in_specs.append(vec)
        args.append(post_g)
    return pl.pallas_call(
        functools.partial(_ffn_kernel, post=post, nf=dff // tf),
        out_shape=jax.ShapeDtypeStruct((n, d), _F32),
        grid=(n // tm, dff // tf),
        in_specs=in_specs,
        out_specs=row,
        scratch_shapes=[pltpu.VMEM((tm, d), _BF16)],
        compiler_params=_params("parallel", "arbitrary"),
        name="ffn_post" if post else "ffn",
    )(*args)


def _inproj_kernel(h_ref, g_ref, w_ref, qg_ref, kg_ref, hsum_ref,
                   q_ref, k_ref, v_ref, u_ref, c_ref, *, aw, sw, cw):
    xn = _rms(h_ref[...], g_ref[...]).astype(_BF16)

    def proj(lo, width):
        return jnp.dot(xn, w_ref[:, lo:lo + width], preferred_element_type=_F32)

    def head_norm(t, gain):
        ss = jnp.dot((t * t).astype(_BF16), hsum_ref[...],
                     preferred_element_type=_F32)
        return t * lax.rsqrt(ss * (1.0 / HEAD_DIM) + RMS_EPS) * gain

    q_ref[...] = head_norm(proj(0, aw), qg_ref[...]).astype(_BF16)
    k_ref[...] = head_norm(proj(aw, aw), kg_ref[...]).astype(_BF16)
    v_ref[...] = proj(2 * aw, aw).astype(_BF16)
    u_ref[...] = proj(3 * aw, sw)
    a = proj(3 * aw + sw, cw)
    gt = proj(3 * aw + sw + cw, cw)
    c_ref[...] = a * jax.nn.sigmoid(gt)


def _inproj(h, g, w, qg, kg, hsum, aw, sw, cw):
    n, d = h.shape
    tm = _pick_tile(n, 640, SUBLANES)
    row = lambda width: pl.BlockSpec((tm, width), lambda i: (i, 0))
    return pl.pallas_call(
        functools.partial(_inproj_kernel, aw=aw, sw=sw, cw=cw),
        out_shape=(jax.ShapeDtypeStruct((n, aw), _BF16),
                   jax.ShapeDtypeStruct((n, aw), _BF16),
                   jax.ShapeDtypeStruct((n, aw), _BF16),
                   jax.ShapeDtypeStruct((n, sw), _F32),
                   jax.ShapeDtypeStruct((n, cw), _F32)),
        grid=(n // tm,),
        in_specs=[row(d), _resident((1, d)), _resident(w.shape),
                  _resident((1, aw)), _resident((1, aw)), _resident((aw, aw))],
        out_specs=(row(aw), row(aw), row(aw), row(sw), row(cw)),
        compiler_params=_params("parallel"),
        name="inproj",
    )(h, g, w, qg, kg, hsum)


def _attn_kernel(q_ref, k_ref, v_ref, tri_ref, o_ref):
    tq = ATT_BLOCK
    qi = pl.program_id(2)
    q = q_ref[...]
    lane = lax.broadcasted_iota(jnp.int32, (tq, LANES), 1)
    first = lane < HEAD_DIM
    zero = jnp.zeros_like(q)
    q_heads = (jnp.where(first, q, zero), jnp.where(first, zero, q))
    tri = tri_ref[...]
    row_id = lax.broadcasted_iota(jnp.int32, (tq, tq), 0)
    col_id = lax.broadcasted_iota(jnp.int32, (tq, tq), 1)
    earlier = col_id < row_id

    def block(j, carry, masked):
        start = pl.multiple_of(j * tq, tq)
        kb = k_ref[pl.ds(start, tq), :]
        vb = v_ref[pl.ds(start, tq), :]
        new = []
        for hd in range(2):
            acc, run = carry[2 * hd], carry[2 * hd + 1]
            z = lax.dot_general(q_heads[hd], kb, (((1,), (1,)), ((), ())),
                                preferred_element_type=_F32)
            log_keep = -(jnp.maximum(z, 0.0) + jnp.log(1.0 + jnp.exp(-jnp.abs(z))))
            if masked:
                log_keep = jnp.where(earlier, log_keep, 0.0)
            hi = log_keep.astype(_BF16)
            lo = (log_keep - hi.astype(_F32)).astype(_BF16)
            survive = (jnp.dot(hi, tri, preferred_element_type=_F32)
                       + jnp.dot(lo, tri, preferred_element_type=_F32))
            att = jnp.exp(log_keep + z + survive + run)
            if masked:
                att = jnp.where(earlier, att, 0.0)
            acc = acc + jnp.dot(att.astype(_BF16), vb, preferred_element_type=_F32)
            run = run + jnp.sum(log_keep, axis=-1, keepdims=True)
            new += [acc, run]
        return tuple(new)

    init = (jnp.zeros((tq, LANES), _F32), jnp.zeros((tq, 1), _F32)) * 2
    carry = block(qi, init, True)
    carry = lax.fori_loop(0, qi, lambda n, c: block(qi - 1 - n, c, False), carry)
    o_ref[...] = jnp.where(first, carry[0], carry[2]).astype(o_ref.dtype)


def _attention(q, k, v, tri):
    b, p, aw = q.shape
    tq = ATT_BLOCK
    blk = pl.BlockSpec((None, tq, LANES), lambda bi, hp, i: (bi, i, hp))
    full = pl.BlockSpec((None, p, LANES), lambda bi, hp, i: (bi, 0, hp))
    return pl.pallas_call(
        _attn_kernel,
        out_shape=jax.ShapeDtypeStruct((b, p, aw), _BF16),
        grid=(b, aw // LANES, p // tq),
        in_specs=[blk, full, full, _resident((tq, tq))],
        out_specs=blk,
        compiler_params=_params("parallel", "parallel", "arbitrary"),
        name="stickbreak_attn",
    )(q, k, v, tri)


def _ssm_kernel(u_ref, kmat_ref, bre_ref, bim_ref, ere_ref, eim_ref,
                are_ref, aim_ref, y_ref, sre_ref, sim_ref, *, nbatch):
    hp = lax.Precision.HIGHEST
    u = u_ref[...]
    sre_ref[...] = jnp.dot(u, bre_ref[...], precision=hp, preferred_element_type=_F32)
    sim_ref[...] = jnp.dot(u, bim_ref[...], precision=hp, preferred_element_type=_F32)

    rows = sre_ref.shape[0]
    width = sre_ref.shape[1]
    a_re = jnp.broadcast_to(are_ref[...], (SUBLANES, width))
    a_im = jnp.broadcast_to(aim_ref[...], (SUBLANES, width))
    chunks_per_tile = SUBLANES // nbatch
    sub = lax.broadcasted_iota(jnp.int32, (SUBLANES, width), 0)

    def step(i, carry):
        s_re, s_im = carry
        r0 = pl.multiple_of(i * SUBLANES, SUBLANES)
        x_re = sre_ref[pl.ds(r0, SUBLANES), :]
        x_im = sim_ref[pl.ds(r0, SUBLANES), :]
        in_re, in_im = s_re, s_im
        for c in range(chunks_per_tile):
            n_re = a_re * s_re - a_im * s_im + x_re
            n_im = a_re * s_im + a_im * s_re + x_im
            s_re = pltpu.roll(n_re, nbatch, 0)
            s_im = pltpu.roll(n_im, nbatch, 0)
            if c + 1 < chunks_per_tile:
                later = sub >= (c + 1) * nbatch
                in_re = jnp.where(later, s_re, in_re)
                in_im = jnp.where(later, s_im, in_im)
        sre_ref[pl.ds(r0, SUBLANES), :] = in_re
        sim_ref[pl.ds(r0, SUBLANES), :] = in_im
        first = sub < nbatch
        out_re, out_im = s_re, s_im
        for c in range(1, chunks_per_tile):
            out_re = jnp.where(first, out_re, pltpu.roll(out_re, nbatch, 0))
            out_im = jnp.where(first, out_im, pltpu.roll(out_im, nbatch, 0))
        return out_re, out_im

    zero = jnp.zeros((SUBLANES, width), _F32)
    lax.fori_loop(0, rows // SUBLANES, step, (zero, zero))

    y = jnp.dot(u, kmat_ref[...], precision=hp, preferred_element_type=_F32)
    y += jnp.dot(sre_ref[...], ere_ref[...], precision=hp, preferred_element_type=_F32)
    y += jnp.dot(sim_ref[...], eim_ref[...], precision=hp, preferred_element_type=_F32)
    y_ref[...] = y


def _ssm_matrices(lam_re, lam_im, log_step, b_re, b_im, c_re, c_im, d_skip):
    t = SSM_CHUNK
    gps = SSM_GROUPS_PER_STEP
    g, p = lam_re.shape
    hg = b_re.shape[-1]
    hp = lax.Precision.HIGHEST
    lam = lax.complex(jnp.minimum(lam_re.astype(_F32), LAMBDA_RE_MAX), lam_im.astype(_F32))
    lam_dt = lam * jnp.exp(log_step.astype(_F32))[:, None]
    lam_bar = jnp.exp(lam_dt)
    b_bar = ((lam_bar - 1.0) / lam)[:, :, None] * lax.complex(b_re.astype(_F32), b_im.astype(_F32))
    c_cplx = lax.complex(c_re.astype(_F32), c_im.astype(_F32))
    steps = jnp.arange(t + 1, dtype=_F32)
    powers = jnp.exp(lam_dt[None] * steps[:, None, None])

    ktau = jnp.real(jnp.einsum('gop,tgp,gpi->gtio', c_cplx, powers[:t], b_bar, precision=hp))
    lag = jnp.arange(t)[None, :] - jnp.arange(t)[:, None]
    kbig = jnp.where((lag >= 0)[None, :, :, None, None],
                     ktau[:, jnp.clip(lag, 0, t - 1)], 0.0)
    kbig = kbig.transpose(0, 1, 3, 2, 4).reshape(g, t * hg, t * hg)
    skip = jnp.tile(d_skip.astype(_F32), (1, t))
    kbig = kbig + skip[:, None, :] * jnp.eye(t * hg, dtype=_F32)[None]

    bfull = (powers[:t][::-1].transpose(1, 0, 2)[:, :, None, :]
             * b_bar.transpose(0, 2, 1)[:, None, :, :]).reshape(g, t * hg, p)
    efull = (c_cplx.transpose(0, 2, 1)[:, :, None, :]
             * powers[1:].transpose(1, 2, 0)[:, :, :, None]).reshape(g, p, t * hg)

    def blockdiag(m):
        gb = g // gps
        r, c = m.shape[1], m.shape[2]
        m = m.reshape(gb, gps, r, c)
        eye = jnp.eye(gps, dtype=m.dtype)
        return jnp.einsum('bgrc,gh->bgrhc', m, eye).reshape(gb, gps * r, gps * c)

    a_t = powers[t].reshape(g // gps, 1, gps * p)
    return (blockdiag(kbig), blockdiag(jnp.real(bfull)), blockdiag(jnp.imag(bfull)),
            blockdiag(jnp.real(efull)), blockdiag(-jnp.imag(efull)),
            jnp.real(a_t), jnp.imag(a_t))


def _ssm(u, mats):
    b, p, sw = u.shape
    t = SSM_CHUNK
    g = sw // SSM_GROUP
    nc = p // t
    assert SUBLANES % b == 0
    kmat, bre, bim, ere, eim, are, aim = mats
    nblk, lw = kmat.shape[0], kmat.shape[1]
    width = bre.shape[2]
    rows = nc * b
    u3 = (u.reshape(b, nc, t, g, SSM_GROUP).transpose(1, 0, 3, 2, 4)
          .reshape(rows, g * t * SSM_GROUP))
    col = pl.BlockSpec((rows, lw), lambda j: (0, j))
    per = lambda r, c: pl.BlockSpec((None, r, c), lambda j: (j, 0, 0))
    y3 = pl.pallas_call(
        functools.partial(_ssm_kernel, nbatch=b),
        out_shape=jax.ShapeDtypeStruct(u3.shape, _F32),
        grid=(nblk,),
        in_specs=[col, per(lw, lw), per(lw, width), per(lw, width),
                  per(width, lw), per(width, lw), per(1, width), per(1, width)],
        out_specs=col,
        scratch_shapes=[pltpu.VMEM((rows, width), _F32),
                        pltpu.VMEM((rows, width), _F32)],
        compiler_params=_params("parallel"),
        name="s5_ssm",
    )(u3, kmat, bre, bim, ere, eim, are, aim)
    return (y3.reshape(nc, b, g, t, SSM_GROUP).transpose(1, 0, 3, 2, 4)
            .reshape(b, p, sw))


def _conv_kernel(halo_ref, x_ref, w_ref, b_ref, lg_ref, lb_ref, o_ref,
                 sh_ref, acc_ref, *, taps):
    tt, c = x_ref.shape
    halo = DW_TAPS_HALO
    lead = halo - (taps - 1)
    first_tile = pl.program_id(1) == 0
    sh_ref[0, 0:halo, :] = jnp.where(first_tile, 0.0, halo_ref[...])
    sh_ref[0, halo:halo + tt, :] = x_ref[...]
    body_rows = tt + halo - SUBLANES
    for s in range(1, SUBLANES):
        sh_ref[s, 0:body_rows, :] = sh_ref[0, s:s + body_rows, :]

    rows = 2 * SUBLANES
    bias = jnp.broadcast_to(b_ref[...], (rows, c))

    def tile(r, _):
        r0 = pl.multiple_of(r * rows, rows)
        acc = bias
        for k in range(taps):
            a, s = divmod(lead + k, SUBLANES)
            acc = acc + w_ref[k:k + 1, :] * sh_ref[s, pl.ds(r0 + a * SUBLANES, rows), :]
        acc_ref[pl.ds(r0, rows), :] = acc
        return 0

    lax.fori_loop(0, tt // rows, tile, 0)

    y = acc_ref[...]
    yc = y - jnp.mean(y, axis=-1, keepdims=True)
    yn = yc * lax.rsqrt(jnp.mean(yc * yc, axis=-1, keepdims=True) + LN_EPS)
    yn = yn * lg_ref[...] + lb_ref[...]
    o_ref[...] = (yn * jax.nn.sigmoid(yn)).astype(o_ref.dtype)


def _conv(x, w, bias, ln_g, ln_b):
    b, p, c = x.shape
    taps = w.shape[0]
    halo = DW_TAPS_HALO
    assert taps - 1 <= halo
    tt = _pick_tile(p, 320, halo)
    per_tile = tt // halo
    vec = lambda r: pl.BlockSpec((r, c), lambda bi, i: (0, 0))
    return pl.pallas_call(
        functools.partial(_conv_kernel, taps=taps),
        out_shape=jax.ShapeDtypeStruct((b, p, c), _BF16),
        grid=(b, p // tt),
        in_specs=[pl.BlockSpec((None, halo, c),
                               lambda bi, i: (bi, jnp.maximum(i * per_tile - 1, 0), 0)),
                  pl.BlockSpec((None, tt, c), lambda bi, i: (bi, i, 0)),
                  vec(taps), vec(1), vec(1), vec(1)],
        out_specs=pl.BlockSpec((None, tt, c), lambda bi, i: (bi, i, 0)),
        scratch_shapes=[pltpu.VMEM((SUBLANES, tt + halo, c), _F32),
                        pltpu.VMEM((tt, c), _F32)],
        compiler_params=_params("parallel", "arbitrary"),
        name="dwconv_ln_swish",
    )(x, x, w, bias, ln_g, ln_b)


def _gelu_tanh(x):
    c = 0.7978845608028654
    return 0.5 * x * (1.0 + jnp.tanh(c * (x + 0.044715 * (x * x * x))))


def _outproj_kernel(h_ref, a_ref, y_ref, c_ref, wglu_ref, bglu_ref, wpw_ref,
                    bpw_ref, wo_ref, o_ref, *, aw, sw):
    y = _gelu_tanh(y_ref[...])
    gate = jnp.dot(y.astype(_BF16), wglu_ref[...], preferred_element_type=_F32) + bglu_ref[...]
    ssm = y * jax.nn.sigmoid(gate)
    conv = jnp.dot(c_ref[...], wpw_ref[...], preferred_element_type=_F32) + bpw_ref[...]
    mixed = jnp.dot(a_ref[...], wo_ref[0:aw, :], preferred_element_type=_F32)
    mixed += jnp.dot(ssm.astype(_BF16), wo_ref[aw:aw + sw, :], preferred_element_type=_F32)
    mixed += jnp.dot(conv.astype(_BF16), wo_ref[aw + sw:, :], preferred_element_type=_F32)
    o_ref[...] = h_ref[...] + mixed


def _outproj(h, attn, y, conv, wglu, bglu, wpw, bpw, wo):
    n, d = h.shape
    aw, sw, cw = attn.shape[1], y.shape[1], conv.shape[1]
    tm = _pick_tile(n, 640, SUBLANES)
    row = lambda width: pl.BlockSpec((tm, width), lambda i: (i, 0))
    return pl.pallas_call(
        functools.partial(_outproj_kernel, aw=aw, sw=sw),
        out_shape=jax.ShapeDtypeStruct((n, d), _F32),
        grid=(n // tm,),
        in_specs=[row(d), row(aw), row(sw), row(cw),
                  _resident(wglu.shape), _resident((1, sw)),
                  _resident(wpw.shape), _resident((1, cw)), _resident(wo.shape)],
        out_specs=row(d),
        compiler_params=_params("parallel"),
        name="outproj",
    )(h, attn, y, conv, wglu, bglu, wpw, bpw, wo)


def kernel(x, meta, ffn1_norm, ffn1_w_gate, ffn1_w_up, ffn1_w_down, mix_norm, w_in, q_norm, k_norm, ssm_lambda_re, ssm_lambda_im, ssm_log_step, ssm_b_re, ssm_b_im, ssm_c_re, ssm_c_im, ssm_d, ssm_w_glu, ssm_b_glu, conv_w_dw, conv_b_dw, conv_ln_g, conv_ln_b, conv_w_pw, conv_b_pw, w_out, ffn2_norm, ffn2_w_gate, ffn2_w_up, ffn2_w_down, post_norm):
    batch, seq, d = x.shape
    depth = ffn1_norm.shape[0]
    n_meta = meta.shape[0]
    length = seq + n_meta
    sw = ssm_w_glu.shape[1]
    cw = conv_w_pw.shape[1]
    aw = (w_in.shape[2] - sw - 2 * cw) // 3
    heads = aw // HEAD_DIM
    assert aw % LANES == 0 and sw % (SSM_GROUP * SSM_GROUPS_PER_STEP) == 0

    p = -(-length // ATT_BLOCK) * ATT_BLOCK
    meta_b = jnp.broadcast_to(meta.astype(x.dtype)[None], (batch, n_meta, d))
    h = jnp.concatenate([meta_b, x, jnp.zeros((batch, p - length, d), x.dtype)], axis=1)
    h = h.reshape(batch * p, d)
    n = batch * p

    head_sum = jnp.kron(jnp.eye(heads, dtype=_F32), jnp.ones((HEAD_DIM, HEAD_DIM), _F32)).astype(_BF16)
    ids = jnp.arange(ATT_BLOCK)
    tri = (ids[:, None] > ids[None, :]).astype(_BF16)
    scale = HEAD_DIM ** -0.5
    vec = lambda a: a.astype(_F32).reshape(1, -1)

    for l in range(depth):
        h = _ffn(h, vec(ffn1_norm[l]), ffn1_w_gate[l].astype(_BF16),
                 ffn1_w_up[l].astype(_BF16), ffn1_w_down[l].astype(_BF16))

        q, k, v, u, ch = _inproj(
            h, vec(mix_norm[l]), w_in[l].astype(_BF16),
            vec(jnp.tile(q_norm[l].astype(_F32) * scale, heads)),
            vec(jnp.tile(k_norm[l].astype(_F32), heads)), head_sum, aw, sw, cw)

        attn = _attention(q.reshape(batch, p, aw), k.reshape(batch, p, aw),
                          v.reshape(batch, p, aw), tri)
        mats = _ssm_matrices(ssm_lambda_re[l], ssm_lambda_im[l], ssm_log_step[l],
                             ssm_b_re[l], ssm_b_im[l], ssm_c_re[l], ssm_c_im[l], ssm_d[l])
        y = _ssm(u.reshape(batch, p, sw), mats)
        conv = _conv(ch.reshape(batch, p, cw), conv_w_dw[l].astype(_F32),
                     vec(conv_b_dw[l]), vec(conv_ln_g[l]), vec(conv_ln_b[l]))

        h = _outproj(h, attn.reshape(n, aw), y.reshape(n, sw), conv.reshape(n, cw),
                     ssm_w_glu[l].astype(_BF16), vec(ssm_b_glu[l]),
                     conv_w_pw[l].astype(_BF16), vec(conv_b_pw[l]),
                     w_out[l].astype(_BF16))

        h = _ffn(h, vec(ffn2_norm[l]), ffn2_w_gate[l].astype(_BF16),
                 ffn2_w_up[l].astype(_BF16), ffn2_w_down[l].astype(_BF16),
                 post_g=vec(post_norm[l]))

    return h.reshape(batch, p, d)[:, n_meta:length]
```

```python
import functools
import math

import jax
import jax.numpy as jnp
from jax import lax
from jax.experimental import pallas as pl
from jax.experimental.pallas import tpu as pltpu

_F32 = jnp.float32
_BF16 = jnp.bfloat16

HEAD_DIM = 64
SSM_GROUP = 16
DW_TAPS_HALO = 32
RMS_EPS = 1e-6
LN_EPS = 1e-5
LAMBDA_RE_MAX = -1e-4
LANES = 128
SUBLANES = 8
SSM_CHUNK = SUBLANES
ATT_BLOCK = 128
ATT_LOG2_MASS_FLOOR = -40.0 * math.log2(math.e)
VMEM_LIMIT_BYTES = 56 * 1024 * 1024


def _pick_tile(n, target, mult):
    best = None
    for t in range(mult, min(n, target) + 1, mult):
        if n % t == 0:
            best = t
    assert best is not None, (n, target, mult)
    return best


def _params(*sem):
    return pltpu.CompilerParams(dimension_semantics=sem,
                                vmem_limit_bytes=VMEM_LIMIT_BYTES)


def _rms(x, g):
    ms = jnp.mean(x * x, axis=-1, keepdims=True)
    return x * lax.rsqrt(ms + RMS_EPS) * g


def _resident(shape):
    zeros = (0,) * len(shape)
    return pl.BlockSpec(shape, lambda *_: zeros, pipeline_mode=pl.Buffered(1))


def _ffn_kernel(*refs, post, nf):
    if post:
        h_ref, g_ref, wg_ref, wu_ref, wd_ref, pg_ref, o_ref, xn_ref = refs
    else:
        h_ref, g_ref, wg_ref, wu_ref, wd_ref, o_ref, xn_ref = refs
    f = pl.program_id(1)

    @pl.when(f == 0)
    def _():
        x = h_ref[...]
        xn_ref[...] = _rms(x, g_ref[...]).astype(_BF16)
        o_ref[...] = x

    xn = xn_ref[...]
    gate = jnp.dot(xn, wg_ref[...], preferred_element_type=_F32)
    up = jnp.dot(xn, wu_ref[...], preferred_element_type=_F32)
    act = (gate * jax.nn.sigmoid(gate)) * (up * 0.5)
    o_ref[...] += jnp.dot(act.astype(_BF16), wd_ref[...],
                          preferred_element_type=_F32)

    if post:
        @pl.when(f == nf - 1)
        def _():
            o_ref[...] = _rms(o_ref[...], pg_ref[...])


def _ffn(h, g, wg, wu, wd, post_g=None):
    n, d = h.shape
    dff = wg.shape[1]
    tm = _pick_tile(n, 640, SUBLANES)
    tf = _pick_tile(dff, 512, LANES)
    post = post_g is not None
    row = pl.BlockSpec((tm, d), lambda i, j: (i, 0))
    vec = pl.BlockSpec((1, d), lambda i, j: (0, 0))
    in_specs = [row, vec,
                pl.BlockSpec((d, tf), lambda i, j: (0, j)),
                pl.BlockSpec((d, tf), lambda i, j: (0, j)),
                pl.BlockSpec((tf, d), lambda i, j: (j, 0))]
    args = [h, g, wg, wu, wd]
    if post:
        in_specs.append(vec)
        args.append(post_g)
    return pl.pallas_call(
        functools.partial(_ffn_kernel, post=post, nf=dff // tf),
        out_shape=jax.ShapeDtypeStruct((n, d), _F32),
        grid=(n // tm, dff // tf),
        in_specs=in_specs,
        out_specs=row,
        scratch_shapes=[pltpu.VMEM((tm, d), _BF16)],
        compiler_params=_params("parallel", "arbitrary"),
        name="ffn_post" if post else "ffn",
    )(*args)


def _inproj_kernel(h_ref, g_ref, w_ref, qg_ref, kg_ref, hsum_ref,
                   q_ref, k_ref, v_ref, u_ref, c_ref, *, aw, sw, cw):
    xn = _rms(h_ref[...], g_ref[...]).astype(_BF16)

    def proj(lo, width):
        return jnp.dot(xn, w_ref[:, lo:lo + width], preferred_element_type=_F32)

    def head_norm(t, gain):
        ss = jnp.dot((t * t).astype(_BF16), hsum_ref[...],
                     preferred_element_type=_F32)
        return t * lax.rsqrt(ss * (1.0 / HEAD_DIM) + RMS_EPS) * gain

    q_ref[...] = head_norm(proj(0, aw), qg_ref[...]).astype(_BF16)
    k_ref[...] = head_norm(proj(aw, aw), kg_ref[...]).astype(_BF16)
    v_ref[...] = proj(2 * aw, aw).astype(_BF16)
    u_ref[...] = proj(3 * aw, sw)
    a = proj(3 * aw + sw, cw)
    gt = proj(3 * aw + sw + cw, cw)
    c_ref[...] = a * jax.nn.sigmoid(gt)


def _inproj(h, g, w, qg, kg, hsum, aw, sw, cw):
    n, d = h.shape
    tm = _pick_tile(n, 640, SUBLANES)
    row = lambda width: pl.BlockSpec((tm, width), lambda i: (i, 0))
    return pl.pallas_call(
        functools.partial(_inproj_kernel, aw=aw, sw=sw, cw=cw),
        out_shape=(jax.ShapeDtypeStruct((n, aw), _BF16),
                   jax.ShapeDtypeStruct((n, aw), _BF16),
                   jax.ShapeDtypeStruct((n, aw), _BF16),
                   jax.ShapeDtypeStruct((n, sw), _F32),
                   jax.ShapeDtypeStruct((n, cw), _F32)),
        grid=(n // tm,),
        in_specs=[row(d), _resident((1, d)), _resident(w.shape),
                  _resident((1, aw)), _resident((1, aw)), _resident((aw, aw))],
        out_specs=(row(aw), row(aw), row(aw), row(sw), row(cw)),
        compiler_params=_params("parallel"),
        name="inproj",
    )(h, g, w, qg, kg, hsum)


def _attn_kernel(q_ref, k_ref, v_ref, tri_ref, o_ref, q2_ref, acc_ref, run_ref):
    tq = ATT_BLOCK
    npairs = q_ref.shape[1] // LANES
    qi = pl.program_id(1)
    first = lax.broadcasted_iota(jnp.int32, (tq, LANES), 1) < HEAD_DIM
    row_id = lax.broadcasted_iota(jnp.int32, (2 * tq, tq), 0)
    col_id = lax.broadcasted_iota(jnp.int32, (2 * tq, tq), 1)
    earlier = col_id < jnp.where(row_id >= tq, row_id - tq, row_id)

    for p in range(npairs):
        q = q_ref[:, p * LANES:(p + 1) * LANES]
        zero = jnp.zeros_like(q)
        q2_ref[p] = jnp.concatenate(
            [jnp.where(first, q, zero), jnp.where(first, zero, q)], axis=0)
    acc_ref[...] = jnp.zeros_like(acc_ref)
    run_ref[...] = jnp.zeros_like(run_ref)

    def sweep(j, masked):
        start = pl.multiple_of(j * tq, tq)
        for p in range(npairs):
            lanes = slice(p * LANES, (p + 1) * LANES)
            kb = k_ref[pl.ds(start, tq), lanes]
            vb = v_ref[pl.ds(start, tq), lanes]
            z = lax.dot_general(q2_ref[p], kb, (((1,), (1,)), ((), ())),
                                preferred_element_type=_F32)
            nz = -z
            log_keep = jnp.minimum(nz, 0.0) - jnp.log2(1.0 + jnp.exp2(jnp.minimum(z, nz)))
            if masked:
                log_keep = jnp.where(earlier, log_keep, 0.0)
            hi = log_keep.astype(_BF16)
            lo = (log_keep - hi.astype(_F32)).astype(_BF16)
            sums = (jnp.dot(hi, tri_ref[...], preferred_element_type=_F32)
                    + jnp.dot(lo, tri_ref[...], preferred_element_type=_F32))
            run = run_ref[p]
            att = jnp.exp2(log_keep + z + sums[:, :tq] + run)
            if masked:
                att = jnp.where(earlier, att, 0.0)
            run_ref[p] = run + sums[:, tq:]
            att2 = jnp.concatenate([att[:tq], att[tq:]], axis=1).astype(_BF16)
            vzero = jnp.zeros_like(vb)
            v2 = jnp.concatenate(
                [jnp.where(first, vb, vzero), jnp.where(first, vzero, vb)], axis=0)
            acc_ref[:, lanes] += jnp.dot(att2, v2, preferred_element_type=_F32)

    sweep(qi, True)

    def more(state):
        j, live = state
        return jnp.logical_and(j >= 0, live)

    def step(state):
        j, _ = state
        sweep(j, False)
        return j - 1, jnp.max(run_ref[...]) > ATT_LOG2_MASS_FLOOR

    lax.while_loop(more, step, (qi - 1, qi >= 0))
    o_ref[...] = acc_ref[...].astype(o_ref.dtype)


def _attention(q, k, v, tri):
    b, p, aw = q.shape
    tq = ATT_BLOCK
    npairs = aw // LANES
    blk = pl.BlockSpec((None, tq, aw), lambda bi, i: (bi, i, 0))
    full = pl.BlockSpec((None, p, aw), lambda bi, i: (bi, 0, 0))
    return pl.pallas_call(
        _attn_kernel,
        out_shape=jax.ShapeDtypeStruct((b, p, aw), _BF16),
        grid=(b, p // tq),
        in_specs=[blk, full, full, _resident(tri.shape)],
        out_specs=blk,
        scratch_shapes=[pltpu.VMEM((npairs, 2 * tq, LANES), _BF16),
                        pltpu.VMEM((tq, aw), _F32),
                        pltpu.VMEM((npairs, 2 * tq, LANES), _F32)],
        compiler_params=_params("parallel", "arbitrary"),
        name="stickbreak_attn",
    )(q, k, v, tri)


def _ssm_kernel(u_ref, kmat_ref, bmat_ref, emat_ref, apow_ref, d_ref, y_ref,
                uc_ref, s_ref):
    t = SSM_CHUNK
    nc = uc_ref.shape[0]
    half = s_ref.shape[1] // 2

    for k in range(t):
        uc_ref[:, k * LANES:(k + 1) * LANES] = u_ref[pl.ds(k, nc, stride=t), :].astype(_BF16)
    s_ref[...] = jnp.dot(uc_ref[...], bmat_ref[...], preferred_element_type=_F32)

    pw_re, pw_im = apow_ref[0], apow_ref[1]
    sub = lax.broadcasted_iota(jnp.int32, (SUBLANES, half), 0)

    def cmul(ar, ai, xr, xi):
        return ar * xr - ai * xi, ar * xi + ai * xr

    def shifted(x, k):
        return jnp.where(sub >= k, pltpu.roll(x, k, 0), 0.0)

    def bcast_row(x, r):
        return jnp.broadcast_to(x[r:r + 1, :], x.shape)

    levels = []
    for k in (1, 2, 4):
        levels.append((k, bcast_row(pw_re, k - 1), bcast_row(pw_im, k - 1)))

    def step(i, carry):
        in_re, in_im = carry
        r0 = pl.multiple_of(i * SUBLANES, SUBLANES)
        p_re = s_ref[pl.ds(r0, SUBLANES), 0:half]
        p_im = s_ref[pl.ds(r0, SUBLANES), half:2 * half]
        for k, ar, ai in levels:
            d_re, d_im = cmul(ar, ai, shifted(p_re, k), shifted(p_im, k))
            p_re, p_im = p_re + d_re, p_im + d_im
        c_re, c_im = cmul(pw_re, pw_im, in_re, in_im)
        out_re, out_im = p_re + c_re, p_im + c_im
        ent_re = jnp.where(sub >= 1, pltpu.roll(out_re, 1, 0), in_re)
        ent_im = jnp.where(sub >= 1, pltpu.roll(out_im, 1, 0), in_im)
        s_ref[pl.ds(r0, SUBLANES), 0:half] = ent_re
        s_ref[pl.ds(r0, SUBLANES), half:2 * half] = ent_im
        return bcast_row(out_re, SUBLANES - 1), bcast_row(out_im, SUBLANES - 1)

    zero = jnp.zeros((SUBLANES, half), _F32)
    lax.fori_loop(0, nc // SUBLANES, step, (zero, zero))

    y = jnp.dot(uc_ref[...], kmat_ref[...], preferred_element_type=_F32)
    y += jnp.dot(s_ref[...].astype(_BF16), emat_ref[...], preferred_element_type=_F32)
    for k in range(t):
        rows = pl.ds(k, nc, stride=t)
        y_ref[rows, :] = y[:, k * LANES:(k + 1) * LANES] + d_ref[...] * u_ref[rows, :]


def _ssm_matrices(lam_re, lam_im, log_step, b_re, b_im, c_re, c_im, d_skip):
    t = SSM_CHUNK
    g, p = lam_re.shape
    hg = b_re.shape[-1]
    gl = LANES // hg
    nb = g // gl
    hp = lax.Precision.HIGHEST
    lam = lax.complex(jnp.minimum(lam_re.astype(_F32), LAMBDA_RE_MAX), lam_im.astype(_F32))
    lam_dt = lam * jnp.exp(log_step.astype(_F32))[:, None]
    lam_bar = jnp.exp(lam_dt)
    b_bar = ((lam_bar - 1.0) / lam)[:, :, None] * lax.complex(b_re.astype(_F32), b_im.astype(_F32))
    c_cplx = lax.complex(c_re.astype(_F32), c_im.astype(_F32))
    steps = jnp.arange(t, dtype=_F32)
    powers = jnp.exp(lam_dt[None] * steps[:, None, None])
    eye = jnp.eye(gl, dtype=_F32)

    ktau = jnp.real(jnp.einsum('gop,tgp,gpi->gtio', c_cplx, powers, b_bar, precision=hp))
    lag = jnp.arange(t)[None, :] - jnp.arange(t)[:, None]
    kbig = jnp.where((lag >= 0)[None, :, :, None, None],
                     ktau[:, jnp.clip(lag, 0, t - 1)], 0.0)
    kmat = jnp.einsum('jgabio,gk->jagibko', kbig.reshape(nb, gl, t, t, hg, hg), eye)
    kmat = kmat.reshape(nb, t * LANES, t * LANES)

    bfull = (powers[::-1].transpose(1, 0, 2)[:, :, None, :]
             * b_bar.transpose(0, 2, 1)[:, None, :, :])
    bfull = bfull.reshape(nb, gl, t, hg, p)

    def spread_b(m):
        return jnp.einsum('jgthp,gk->jtghkp', m, eye).reshape(nb, t * LANES, gl * p)
    bmat = jnp.concatenate([spread_b(jnp.real(bfull)), spread_b(jnp.imag(bfull))], axis=2)

    efull = (c_cplx.transpose(0, 2, 1)[:, :, None, :]
             * (powers * lam_bar[None]).transpose(1, 2, 0)[:, :, :, None])
    efull = efull.reshape(nb, gl, p, t, hg)

    def spread_e(m):
        return jnp.einsum('jgpth,gk->jgptkh', m, eye).reshape(nb, gl * p, t * LANES)
    emat = jnp.concatenate([spread_e(jnp.real(efull)), spread_e(-jnp.imag(efull))], axis=1)

    rows = jnp.arange(1, SUBLANES + 1, dtype=_F32) * t
    apow = jnp.exp(lam_dt[None] * rows[:, None, None])
    apow = apow.reshape(SUBLANES, nb, gl * p).transpose(1, 0, 2)
    apow = jnp.stack([jnp.real(apow), jnp.imag(apow)], axis=1)
    return (kmat.astype(_BF16), bmat.astype(_BF16), emat.astype(_BF16), apow,
            d_skip.astype(_F32).reshape(nb, 1, LANES))


def _ssm(u, mats):
    b, p, sw = u.shape
    t = SSM_CHUNK
    nc = p // t
    assert nc % SUBLANES == 0 and sw % LANES == 0
    kmat, bmat, emat, apow, dvec = mats
    nb, lw, sl = bmat.shape
    col = pl.BlockSpec((None, p, LANES), lambda j, bi: (bi, 0, j))
    per = lambda *shape: pl.BlockSpec((None,) + shape, lambda j, bi: (j,) + (0,) * len(shape))
    return pl.pallas_call(
        _ssm_kernel,
        out_shape=jax.ShapeDtypeStruct(u.shape, _F32),
        grid=(nb, b),
        in_specs=[col, per(lw, lw), per(lw, sl), per(sl, lw),
                  per(2, SUBLANES, sl // 2), per(1, LANES)],
        out_specs=col,
        scratch_shapes=[pltpu.VMEM((nc, lw), _BF16), pltpu.VMEM((nc, sl), _F32)],
        compiler_params=_params("parallel", "parallel"),
        name="s5_ssm",
    )(u, kmat, bmat, emat, apow, dvec)


def _conv_kernel(halo_ref, x_ref, w_ref, b_ref, lg_ref, lb_ref, o_ref,
                 sh_ref, acc_ref, *, taps):
    tt, c = x_ref.shape
    halo = DW_TAPS_HALO
    lead = halo - (taps - 1)
    first_tile = pl.program_id(1) == 0
    sh_ref[0, 0:halo, :] = jnp.where(first_tile, 0.0, halo_ref[...])
    sh_ref[0, halo:halo + tt, :] = x_ref[...]
    body_rows = tt + halo - SUBLANES
    for s in range(1, SUBLANES):
        sh_ref[s, 0:body_rows, :] = sh_ref[0, s:s + body_rows, :]

    rows = 2 * SUBLANES
    bias = jnp.broadcast_to(b_ref[...], (rows, c))

    def tile(r, _):
        r0 = pl.multiple_of(r * rows, rows)
        acc = bias
        for k in range(taps):
            a, s = divmod(lead + k, SUBLANES)
            acc = acc + w_ref[k:k + 1, :] * sh_ref[s, pl.ds(r0 + a * SUBLANES, rows), :]
        acc_ref[pl.ds(r0, rows), :] = acc
        return 0

    lax.fori_loop(0, tt // rows, tile, 0)

    y = acc_ref[...]
    yc = y - jnp.mean(y, axis=-1, keepdims=True)
    yn = yc * lax.rsqrt(jnp.mean(yc * yc, axis=-1, keepdims=True) + LN_EPS)
    yn = yn * lg_ref[...] + lb_ref[...]
    o_ref[...] = (yn * jax.nn.sigmoid(yn)).astype(o_ref.dtype)


def _conv(x, w, bias, ln_g, ln_b):
    b, p, c = x.shape
    taps = w.shape[0]
    halo = DW_TAPS_HALO
    assert taps - 1 <= halo
    tt = _pick_tile(p, 320, halo)
    per_tile = tt // halo
    vec = lambda r: pl.BlockSpec((r, c), lambda bi, i: (0, 0))
    return pl.pallas_call(
        functools.partial(_conv_kernel, taps=taps),
        out_shape=jax.ShapeDtypeStruct((b, p, c), _BF16),
        grid=(b, p // tt),
        in_specs=[pl.BlockSpec((None, halo, c),
                               lambda bi, i: (bi, jnp.maximum(i * per_tile - 1, 0), 0)),
                  pl.BlockSpec((None, tt, c), lambda bi, i: (bi, i, 0)),
                  vec(taps), vec(1), vec(1), vec(1)],
        out_specs=pl.BlockSpec((None, tt, c), lambda bi, i: (bi, i, 0)),
        scratch_shapes=[pltpu.VMEM((SUBLANES, tt + halo, c), _F32),
                        pltpu.VMEM((tt, c), _F32)],
        compiler_params=_params("parallel", "arbitrary"),
        name="dwconv_ln_swish",
    )(x, x, w, bias, ln_g, ln_b)


def _gelu_tanh(x):
    c = 0.7978845608028654
    return 0.5 * x * (1.0 + jnp.tanh(c * (x + 0.044715 * (x * x * x))))


def _outproj_kernel(h_ref, a_ref, y_ref, c_ref, wglu_ref, bglu_ref, wpw_ref,
                    bpw_ref, wo_ref, o_ref, *, aw, sw):
    y = _gelu_tanh(y_ref[...])
    gate = jnp.dot(y.astype(_BF16), wglu_ref[...], preferred_element_type=_F32) + bglu_ref[...]
    ssm = y * jax.nn.sigmoid(gate)
    conv = jnp.dot(c_ref[...], wpw_ref[...], preferred_element_type=_F32) + bpw_ref[...]
    mixed = jnp.dot(a_ref[...], wo_ref[0:aw, :], preferred_element_type=_F32)
    mixed += jnp.dot(ssm.astype(_BF16), wo_ref[aw:aw + sw, :], preferred_element_type=_F32)
    mixed += jnp.dot(conv.astype(_BF16), wo_ref[aw + sw:, :], preferred_element_type=_F32)
    o_ref[...] = h_ref[...] + mixed


def _outproj(h, attn, y, conv, wglu, bglu, wpw, bpw, wo):
    n, d = h.shape
    aw, sw, cw = attn.shape[1], y.shape[1], conv.shape[1]
    tm = _pick_tile(n, 640, SUBLANES)
    row = lambda width: pl.BlockSpec((tm, width), lambda i: (i, 0))
    return pl.pallas_call(
        functools.partial(_outproj_kernel, aw=aw, sw=sw),
        out_shape=jax.ShapeDtypeStruct((n, d), _F32),
        grid=(n // tm,),
        in_specs=[row(d), row(aw), row(sw), row(cw),
                  _resident(wglu.shape), _resident((1, sw)),
                  _resident(wpw.shape), _resident((1, cw)), _resident(wo.shape)],
        out_specs=row(d),
        compiler_params=_params("parallel"),
        name="outproj",
    )(h, attn, y, conv, wglu, bglu, wpw, bpw, wo)


def kernel(x, meta, ffn1_norm, ffn1_w_gate, ffn1_w_up, ffn1_w_down, mix_norm, w_in, q_norm, k_norm, ssm_lambda_re, ssm_lambda_im, ssm_log_step, ssm_b_re, ssm_b_im, ssm_c_re, ssm_c_im, ssm_d, ssm_w_glu, ssm_b_glu, conv_w_dw, conv_b_dw, conv_ln_g, conv_ln_b, conv_w_pw, conv_b_pw, w_out, ffn2_norm, ffn2_w_gate, ffn2_w_up, ffn2_w_down, post_norm):
    batch, seq, d = x.shape
    depth = ffn1_norm.shape[0]
    n_meta = meta.shape[0]
    length = seq + n_meta
    sw = ssm_w_glu.shape[1]
    cw = conv_w_pw.shape[1]
    aw = (w_in.shape[2] - sw - 2 * cw) // 3
    heads = aw // HEAD_DIM
    assert aw % LANES == 0 and sw % LANES == 0 and LANES % SSM_GROUP == 0

    p = -(-length // ATT_BLOCK) * ATT_BLOCK
    meta_b = jnp.broadcast_to(meta.astype(x.dtype)[None], (batch, n_meta, d))
    h = jnp.concatenate([meta_b, x, jnp.zeros((batch, p - length, d), x.dtype)], axis=1)
    h = h.reshape(batch * p, d)
    n = batch * p

    head_sum = jnp.kron(jnp.eye(heads, dtype=_F32), jnp.ones((HEAD_DIM, HEAD_DIM), _F32)).astype(_BF16)
    ids = jnp.arange(ATT_BLOCK)
    tri = jnp.concatenate([(ids[:, None] > ids[None, :]).astype(_BF16),
                           jnp.ones((ATT_BLOCK, ATT_BLOCK), _BF16)], axis=1)
    logit_scale = HEAD_DIM ** -0.5 * math.log2(math.e)
    vec = lambda a: a.astype(_F32).reshape(1, -1)

    for l in range(depth):
        h = _ffn(h, vec(ffn1_norm[l]), ffn1_w_gate[l].astype(_BF16),
                 ffn1_w_up[l].astype(_BF16), ffn1_w_down[l].astype(_BF16))

        q, k, v, u, ch = _inproj(
            h, vec(mix_norm[l]), w_in[l].astype(_BF16),
            vec(jnp.tile(q_norm[l].astype(_F32) * logit_scale, heads)),
            vec(jnp.tile(k_norm[l].astype(_F32), heads)), head_sum, aw, sw, cw)

        attn = _attention(q.reshape(batch, p, aw), k.reshape(batch, p, aw),
                          v.reshape(batch, p, aw), tri)
        mats = _ssm_matrices(ssm_lambda_re[l], ssm_lambda_im[l], ssm_log_step[l],
                             ssm_b_re[l], ssm_b_im[l], ssm_c_re[l], ssm_c_im[l], ssm_d[l])
        y = _ssm(u.reshape(batch, p, sw), mats)
        conv = _conv(ch.reshape(batch, p, cw), conv_w_dw[l].astype(_F32),
                     vec(conv_b_dw[l]), vec(conv_ln_g[l]), vec(conv_ln_b[l]))

        h = _outproj(h, attn.reshape(n, aw), y.reshape(n, sw), conv.reshape(n, cw),
                     ssm_w_glu[l].astype(_BF16), vec(ssm_b_glu[l]),
                     conv_w_pw[l].astype(_BF16), vec(conv_b_pw[l]),
                     w_out[l].astype(_BF16))

        h = _ffn(h, vec(ffn2_norm[l]), ffn2_w_gate[l].astype(_BF16),
                 ffn2_w_up[l].astype(_BF16), ffn2_w_down[l].astype(_BF16),
                 post_g=vec(post_norm[l]))

    return h.reshape(batch, p, d)[:, n_meta:length]
```

```python
import functools
import math

import jax
import jax.numpy as jnp
from jax import lax
from jax.experimental import pallas as pl
from jax.experimental.pallas import tpu as pltpu

_F32 = jnp.float32
_BF16 = jnp.bfloat16

HEAD_DIM = 64
SSM_GROUP = 16
DW_TAPS_HALO = 32
RMS_EPS = 1e-6
LN_EPS = 1e-5
LAMBDA_RE_MAX = -1e-4
LANES = 128
SUBLANES = 8
SSM_CHUNK = SUBLANES
ATT_BLOCK = 128
ATT_LOG2_MASS_FLOOR = -40.0 * math.log2(math.e)
VMEM_LIMIT_BYTES = 56 * 1024 * 1024


def _pick_tile(n, target, mult):
    best = None
    for t in range(mult, min(n, target) + 1, mult):
        if n % t == 0:
            best = t
    assert best is not None, (n, target, mult)
    return best


def _params(*sem):
    return pltpu.CompilerParams(dimension_semantics=sem,
                                vmem_limit_bytes=VMEM_LIMIT_BYTES)


def _rms(x, g):
    ms = jnp.mean(x * x, axis=-1, keepdims=True)
    return x * lax.rsqrt(ms + RMS_EPS) * g


def _resident(shape):
    zeros = (0,) * len(shape)
    return pl.BlockSpec(shape, lambda *_: zeros, pipeline_mode=pl.Buffered(1))


_FFN_SOURCE_REFS = {"rows": 1, "embed": 3, "shifted": 2}


def _ffn_kernel(*refs, source, post, nf, n_meta, length):
    n_src = _FFN_SOURCE_REFS[source]
    src = refs[:n_src]
    g_ref, wg_ref, wu_ref, wd_ref = refs[n_src:n_src + 4]
    pg_ref = refs[n_src + 4] if post else None
    o_ref, xn_ref = refs[-2:]
    i = pl.program_id(1)
    f = pl.program_id(2)
    tm = o_ref.shape[0]

    @pl.when(f == 0)
    def _():
        if source == "rows":
            x = src[0][...]
        elif source == "embed":
            tail_ref, body_ref, meta_ref = src
            head = jnp.where(i == 0, meta_ref[...], tail_ref[...])
            x = jnp.concatenate([head, body_ref[0:tm - n_meta, :]], axis=0)
            row = i * tm + lax.broadcasted_iota(jnp.int32, (tm, 1), 0)
            x = jnp.where(row < length, x, 0.0)
        else:
            body_ref, next_ref = src
            x = jnp.concatenate([body_ref[n_meta:, :], next_ref[...]], axis=0)
        xn_ref[...] = _rms(x, g_ref[...]).astype(_BF16)
        o_ref[...] = x

    xn = xn_ref[...]
    gate = jnp.dot(xn, wg_ref[...], preferred_element_type=_F32)
    up = jnp.dot(xn, wu_ref[...], preferred_element_type=_F32)
    act = (gate * jax.nn.sigmoid(gate)) * (up * 0.5)
    o_ref[...] += jnp.dot(act.astype(_BF16), wd_ref[...],
                          preferred_element_type=_F32)

    if post:
        @pl.when(f == nf - 1)
        def _():
            o_ref[...] = _rms(o_ref[...], pg_ref[...])


def _ffn(src, g, wg, wu, wd, post_g=None, source="rows", p=None, seq=None, n_meta=SUBLANES):
    b, rows_in, d = src[0].shape
    p = rows_in if p is None else p
    dff = wg.shape[1]
    assert n_meta % SUBLANES == 0
    tm = _pick_tile(p, 640, n_meta)
    tf = _pick_tile(dff, 512, LANES)
    per_tile = tm // n_meta
    post = post_g is not None
    tile = pl.BlockSpec((None, tm, d), lambda bi, i, j: (bi, i, 0))
    vec = pl.BlockSpec((1, d), lambda bi, i, j: (0, 0))
    length = 0
    if source == "rows":
        out_rows = p
        src_specs, args = [tile], [src[0]]
    elif source == "embed":
        x, meta = src
        length = rows_in + n_meta
        out_rows = p
        last = pl.cdiv(rows_in, tm) - 1
        src_specs = [
            pl.BlockSpec((None, n_meta, d),
                         lambda bi, i, j: (bi, jnp.maximum(i * per_tile - 1, 0), 0)),
            pl.BlockSpec((None, tm, d), lambda bi, i, j: (bi, jnp.minimum(i, last), 0)),
            pl.BlockSpec((n_meta, d), lambda bi, i, j: (0, 0))]
        args = [x, x, meta]
    else:
        out_rows = seq
        last = p // n_meta - 1
        src_specs = [
            tile,
            pl.BlockSpec((None, n_meta, d),
                         lambda bi, i, j: (bi, jnp.minimum((i + 1) * per_tile, last), 0))]
        args = [src[0], src[0]]
    in_specs = src_specs + [vec,
                            pl.BlockSpec((d, tf), lambda bi, i, j: (0, j)),
                            pl.BlockSpec((d, tf), lambda bi, i, j: (0, j)),
                            pl.BlockSpec((tf, d), lambda bi, i, j: (j, 0))]
    args += [g, wg, wu, wd]
    if post:
        in_specs.append(vec)
        args.append(post_g)
    return pl.pallas_call(
        functools.partial(_ffn_kernel, source=source, post=post, nf=dff // tf,
                          n_meta=n_meta, length=length),
        out_shape=jax.ShapeDtypeStruct((b, out_rows, d), _F32),
        grid=(b, pl.cdiv(out_rows, tm), dff // tf),
        in_specs=in_specs,
        out_specs=tile,
        scratch_shapes=[pltpu.VMEM((tm, d), _BF16)],
        compiler_params=_params("parallel", "parallel", "arbitrary"),
        name="ffn_" + source + ("_post" if post else ""),
    )(*args)


def _inproj_kernel(h_ref, g_ref, w_ref, qg_ref, kg_ref, hsum_ref,
                   q_ref, k_ref, v_ref, u_ref, c_ref, *, aw, sw, cw):
    xn = _rms(h_ref[...], g_ref[...]).astype(_BF16)

    def proj(lo, width):
        return jnp.dot(xn, w_ref[:, lo:lo + width], preferred_element_type=_F32)

    def head_norm(t, gain):
        ss = jnp.dot((t * t).astype(_BF16), hsum_ref[...],
                     preferred_element_type=_F32)
        return t * lax.rsqrt(ss * (1.0 / HEAD_DIM) + RMS_EPS) * gain

    q_ref[...] = head_norm(proj(0, aw), qg_ref[...]).astype(_BF16)
    k_ref[...] = head_norm(proj(aw, aw), kg_ref[...]).astype(_BF16)
    v_ref[...] = proj(2 * aw, aw).astype(_BF16)
    u_ref[...] = proj(3 * aw, sw)
    a = proj(3 * aw + sw, cw)
    gt = proj(3 * aw + sw + cw, cw)
    c_ref[...] = a * jax.nn.sigmoid(gt)


def _inproj(h, g, w, qg, kg, hsum, aw, sw, cw):
    n, d = h.shape
    tm = _pick_tile(n, 640, SUBLANES)
    row = lambda width: pl.BlockSpec((tm, width), lambda i: (i, 0))
    return pl.pallas_call(
        functools.partial(_inproj_kernel, aw=aw, sw=sw, cw=cw),
        out_shape=(jax.ShapeDtypeStruct((n, aw), _BF16),
                   jax.ShapeDtypeStruct((n, aw), _BF16),
                   jax.ShapeDtypeStruct((n, aw), _BF16),
                   jax.ShapeDtypeStruct((n, sw), _F32),
                   jax.ShapeDtypeStruct((n, cw), _F32)),
        grid=(n // tm,),
        in_specs=[row(d), _resident((1, d)), _resident(w.shape),
                  _resident((1, aw)), _resident((1, aw)), _resident((aw, aw))],
        out_specs=(row(aw), row(aw), row(aw), row(sw), row(cw)),
        compiler_params=_params("parallel"),
        name="inproj",
    )(h, g, w, qg, kg, hsum)


def _attn_kernel(q_ref, k_ref, v_ref, tri_ref, o_ref, q2_ref, acc_ref, run_ref):
    tq = ATT_BLOCK
    npairs = q_ref.shape[1] // LANES
    qi = pl.program_id(1)
    first = lax.broadcasted_iota(jnp.int32, (tq, LANES), 1) < HEAD_DIM
    row_id = lax.broadcasted_iota(jnp.int32, (2 * tq, tq), 0)
    col_id = lax.broadcasted_iota(jnp.int32, (2 * tq, tq), 1)
    earlier = col_id < jnp.where(row_id >= tq, row_id - tq, row_id)

    for p in range(npairs):
        q = q_ref[:, p * LANES:(p + 1) * LANES]
        zero = jnp.zeros_like(q)
        q2_ref[p] = jnp.concatenate(
            [jnp.where(first, q, zero), jnp.where(first, zero, q)], axis=0)
    acc_ref[...] = jnp.zeros_like(acc_ref)
    run_ref[...] = jnp.zeros_like(run_ref)

    def sweep(j, masked):
        start = pl.multiple_of(j * tq, tq)
        keys = pl.ds(start, tq)
        pair_lanes = [slice(p * LANES, (p + 1) * LANES) for p in range(npairs)]
        tri = tri_ref[...]

        logits = [lax.dot_general(q2_ref[p], k_ref[keys, pair_lanes[p]],
                                  (((1,), (1,)), ((), ())),
                                  preferred_element_type=_F32)
                  for p in range(npairs)]

        partial = []
        for z in logits:
            nz = -z
            log_keep = jnp.minimum(nz, 0.0) - jnp.log2(1.0 + jnp.exp2(jnp.minimum(z, nz)))
            if masked:
                log_keep = jnp.where(earlier, log_keep, 0.0)
            hi = log_keep.astype(_BF16)
            lo = (log_keep - hi.astype(_F32)).astype(_BF16)
            sums = (jnp.dot(hi, tri, preferred_element_type=_F32)
                    + jnp.dot(lo, tri, preferred_element_type=_F32))
            partial.append((log_keep + z, sums))

        mass = None
        for p, (log_hit, sums) in enumerate(partial):
            run = run_ref[p]
            att = jnp.exp2(log_hit + sums[:, :tq] + run)
            if masked:
                att = jnp.where(earlier, att, 0.0)
            att2 = jnp.concatenate([att[:tq], att[tq:]], axis=1).astype(_BF16)
            vb = v_ref[keys, pair_lanes[p]]
            vzero = jnp.zeros_like(vb)
            v2 = jnp.concatenate(
                [jnp.where(first, vb, vzero), jnp.where(first, vzero, vb)], axis=0)
            acc_ref[:, pair_lanes[p]] += jnp.dot(att2, v2, preferred_element_type=_F32)
            run = run + sums[:, tq:]
            run_ref[p] = run
            mass = run if mass is None else jnp.maximum(mass, run)
        return jnp.max(mass)

    sweep(qi, True)

    def more(state):
        j, live = state
        return jnp.logical_and(j >= 0, live)

    def step(state):
        j, _ = state
        return j - 1, sweep(j, False) > ATT_LOG2_MASS_FLOOR

    lax.while_loop(more, step, (qi - 1, qi >= 0))
    o_ref[...] = acc_ref[...].astype(o_ref.dtype)


def _attention(q, k, v, tri):
    b, p, aw = q.shape
    tq = ATT_BLOCK
    npairs = aw // LANES
    blk = pl.BlockSpec((None, tq, aw), lambda bi, i: (bi, i, 0))
    full = pl.BlockSpec((None, p, aw), lambda bi, i: (bi, 0, 0))
    return pl.pallas_call(
        _attn_kernel,
        out_shape=jax.ShapeDtypeStruct((b, p, aw), _BF16),
        grid=(b, p // tq),
        in_specs=[blk, full, full, _resident(tri.shape)],
        out_specs=blk,
        scratch_shapes=[pltpu.VMEM((npairs, 2 * tq, LANES), _BF16),
                        pltpu.VMEM((tq, aw), _F32),
                        pltpu.VMEM((npairs, 2 * tq, LANES), _F32)],
        compiler_params=_params("parallel", "arbitrary"),
        name="stickbreak_attn",
    )(q, k, v, tri)


def _ssm_kernel(u_ref, kmat_ref, bmat_ref, emat_ref, apow_ref, d_ref, y_ref,
                uc_ref, s_ref):
    t = SSM_CHUNK
    nc = uc_ref.shape[0]
    half = s_ref.shape[1] // 2

    for k in range(t):
        uc_ref[:, k * LANES:(k + 1) * LANES] = u_ref[pl.ds(k, nc, stride=t), :].astype(_BF16)
    s_ref[...] = jnp.dot(uc_ref[...], bmat_ref[...], preferred_element_type=_F32)

    pw_re, pw_im = apow_ref[0], apow_ref[1]
    sub = lax.broadcasted_iota(jnp.int32, (SUBLANES, half), 0)

    def cmul(ar, ai, xr, xi):
        return ar * xr - ai * xi, ar * xi + ai * xr

    def shifted(x, k):
        return jnp.where(sub >= k, pltpu.roll(x, k, 0), 0.0)

    def bcast_row(x, r):
        return jnp.broadcast_to(x[r:r + 1, :], x.shape)

    levels = []
    for k in (1, 2, 4):
        levels.append((k, bcast_row(pw_re, k - 1), bcast_row(pw_im, k - 1)))

    def step(i, carry):
        in_re, in_im = carry
        r0 = pl.multiple_of(i * SUBLANES, SUBLANES)
        p_re = s_ref[pl.ds(r0, SUBLANES), 0:half]
        p_im = s_ref[pl.ds(r0, SUBLANES), half:2 * half]
        for k, ar, ai in levels:
            d_re, d_im = cmul(ar, ai, shifted(p_re, k), shifted(p_im, k))
            p_re, p_im = p_re + d_re, p_im + d_im
        c_re, c_im = cmul(pw_re, pw_im, in_re, in_im)
        out_re, out_im = p_re + c_re, p_im + c_im
        ent_re = jnp.where(sub >= 1, pltpu.roll(out_re, 1, 0), in_re)
        ent_im = jnp.where(sub >= 1, pltpu.roll(out_im, 1, 0), in_im)
        s_ref[pl.ds(r0, SUBLANES), 0:half] = ent_re
        s_ref[pl.ds(r0, SUBLANES), half:2 * half] = ent_im
        return bcast_row(out_re, SUBLANES - 1), bcast_row(out_im, SUBLANES - 1)

    zero = jnp.zeros((SUBLANES, half), _F32)
    lax.fori_loop(0, nc // SUBLANES, step, (zero, zero))

    y = jnp.dot(uc_ref[...], kmat_ref[...], preferred_element_type=_F32)
    y += jnp.dot(s_ref[...].astype(_BF16), emat_ref[...], preferred_element_type=_F32)
    for k in range(t):
        rows = pl.ds(k, nc, stride=t)
        y_ref[rows, :] = y[:, k * LANES:(k + 1) * LANES] + d_ref[...] * u_ref[rows, :]


def _ssm_matrices(lam_re, lam_im, log_step, b_re, b_im, c_re, c_im, d_skip):
    t = SSM_CHUNK
    g, p = lam_re.shape
    hg = b_re.shape[-1]
    gl = LANES // hg
    nb = g // gl
    hp = lax.Precision.HIGHEST
    lam = lax.complex(jnp.minimum(lam_re.astype(_F32), LAMBDA_RE_MAX), lam_im.astype(_F32))
    lam_dt = lam * jnp.exp(log_step.astype(_F32))[:, None]
    lam_bar = jnp.exp(lam_dt)
    b_bar = ((lam_bar - 1.0) / lam)[:, :, None] * lax.complex(b_re.astype(_F32), b_im.astype(_F32))
    c_cplx = lax.complex(c_re.astype(_F32), c_im.astype(_F32))
    steps = jnp.arange(t, dtype=_F32)
    powers = jnp.exp(lam_dt[None] * steps[:, None, None])
    eye = jnp.eye(gl, dtype=_F32)

    ktau = jnp.real(jnp.einsum('gop,tgp,gpi->gtio', c_cplx, powers, b_bar, precision=hp))
    lag = jnp.arange(t)[None, :] - jnp.arange(t)[:, None]
    kbig = jnp.where((lag >= 0)[None, :, :, None, None],
                     ktau[:, jnp.clip(lag, 0, t - 1)], 0.0)
    kmat = jnp.einsum('jgabio,gk->jagibko', kbig.reshape(nb, gl, t, t, hg, hg), eye)
    kmat = kmat.reshape(nb, t * LANES, t * LANES)

    bfull = (powers[::-1].transpose(1, 0, 2)[:, :, None, :]
             * b_bar.transpose(0, 2, 1)[:, None, :, :])
    bfull = bfull.reshape(nb, gl, t, hg, p)

    def spread_b(m):
        return jnp.einsum('jgthp,gk->jtghkp', m, eye).reshape(nb, t * LANES, gl * p)
    bmat = jnp.concatenate([spread_b(jnp.real(bfull)), spread_b(jnp.imag(bfull))], axis=2)

    efull = (c_cplx.transpose(0, 2, 1)[:, :, None, :]
             * (powers * lam_bar[None]).transpose(1, 2, 0)[:, :, :, None])
    efull = efull.reshape(nb, gl, p, t, hg)

    def spread_e(m):
        return jnp.einsum('jgpth,gk->jgptkh', m, eye).reshape(nb, gl * p, t * LANES)
    emat = jnp.concatenate([spread_e(jnp.real(efull)), spread_e(-jnp.imag(efull))], axis=1)

    rows = jnp.arange(1, SUBLANES + 1, dtype=_F32) * t
    apow = jnp.exp(lam_dt[None] * rows[:, None, None])
    apow = apow.reshape(SUBLANES, nb, gl * p).transpose(1, 0, 2)
    apow = jnp.stack([jnp.real(apow), jnp.imag(apow)], axis=1)
    return (kmat.astype(_BF16), bmat.astype(_BF16), emat.astype(_BF16), apow,
            d_skip.astype(_F32).reshape(nb, 1, LANES))


def _ssm(u, mats):
    b, p, sw = u.shape
    t = SSM_CHUNK
    nc = p // t
    assert nc % SUBLANES == 0 and sw % LANES == 0
    kmat, bmat, emat, apow, dvec = mats
    nb, lw, sl = bmat.shape
    col = pl.BlockSpec((None, p, LANES), lambda j, bi: (bi, 0, j))
    per = lambda *shape: pl.BlockSpec((None,) + shape, lambda j, bi: (j,) + (0,) * len(shape))
    return pl.pallas_call(
        _ssm_kernel,
        out_shape=jax.ShapeDtypeStruct(u.shape, _F32),
        grid=(nb, b),
        in_specs=[col, per(lw, lw), per(lw, sl), per(sl, lw),
                  per(2, SUBLANES, sl // 2), per(1, LANES)],
        out_specs=col,
        scratch_shapes=[pltpu.VMEM((nc, lw), _BF16), pltpu.VMEM((nc, sl), _F32)],
        compiler_params=_params("parallel", "parallel"),
        name="s5_ssm",
    )(u, kmat, bmat, emat, apow, dvec)


def _conv_kernel(halo_ref, x_ref, w_ref, b_ref, lg_ref, lb_ref, o_ref,
                 sh_ref, acc_ref, wb_ref, *, taps):
    tt, c = x_ref.shape
    halo = DW_TAPS_HALO
    lead = halo - (taps - 1)
    first_tile = pl.program_id(1) == 0
    sh_ref[0, 0:halo, :] = jnp.where(first_tile, 0.0, halo_ref[...])
    sh_ref[0, halo:halo + tt, :] = x_ref[...]
    body_rows = tt + halo - SUBLANES
    for s in range(1, SUBLANES):
        sh_ref[s, 0:body_rows, :] = sh_ref[0, s:s + body_rows, :]

    for k in range(taps):
        wb_ref[k] = jnp.broadcast_to(w_ref[k:k + 1, :], (SUBLANES, c))
    bias = jnp.broadcast_to(b_ref[...], (SUBLANES, c))
    halves = 4
    rows = halves * SUBLANES

    def tile(r, _):
        r0 = pl.multiple_of(r * rows, rows)
        acc = [bias] * halves
        for k in range(taps):
            a, s = divmod(lead + k, SUBLANES)
            w = wb_ref[k]
            for i in range(halves):
                acc[i] = acc[i] + w * sh_ref[s, pl.ds(r0 + (a + i) * SUBLANES, SUBLANES), :]
        for i in range(halves):
            acc_ref[pl.ds(r0 + i * SUBLANES, SUBLANES), :] = acc[i]
        return 0

    lax.fori_loop(0, tt // rows, tile, 0)

    y = acc_ref[...]
    yc = y - jnp.mean(y, axis=-1, keepdims=True)
    yn = yc * lax.rsqrt(jnp.mean(yc * yc, axis=-1, keepdims=True) + LN_EPS)
    yn = yn * lg_ref[...] + lb_ref[...]
    o_ref[...] = (yn * jax.nn.sigmoid(yn)).astype(o_ref.dtype)


def _conv(x, w, bias, ln_g, ln_b):
    b, p, c = x.shape
    taps = w.shape[0]
    halo = DW_TAPS_HALO
    assert taps - 1 <= halo
    tt = _pick_tile(p, 320, halo)
    per_tile = tt // halo
    vec = lambda r: pl.BlockSpec((r, c), lambda bi, i: (0, 0))
    return pl.pallas_call(
        functools.partial(_conv_kernel, taps=taps),
        out_shape=jax.ShapeDtypeStruct((b, p, c), _BF16),
        grid=(b, p // tt),
        in_specs=[pl.BlockSpec((None, halo, c),
                               lambda bi, i: (bi, jnp.maximum(i * per_tile - 1, 0), 0)),
                  pl.BlockSpec((None, tt, c), lambda bi, i: (bi, i, 0)),
                  vec(taps), vec(1), vec(1), vec(1)],
        out_specs=pl.BlockSpec((None, tt, c), lambda bi, i: (bi, i, 0)),
        scratch_shapes=[pltpu.VMEM((SUBLANES, tt + halo, c), _F32),
                        pltpu.VMEM((tt, c), _F32),
                        pltpu.VMEM((taps, SUBLANES, c), _F32)],
        compiler_params=_params("parallel", "arbitrary"),
        name="dwconv_ln_swish",
    )(x, x, w, bias, ln_g, ln_b)


def _gelu_tanh(x):
    c = 0.7978845608028654
    return 0.5 * x * (1.0 + jnp.tanh(c * (x + 0.044715 * (x * x * x))))


def _outproj_kernel(h_ref, a_ref, y_ref, c_ref, wglu_ref, bglu_ref, wpw_ref,
                    bpw_ref, wo_ref, o_ref, *, aw, sw):
    y = _gelu_tanh(y_ref[...])
    gate = jnp.dot(y.astype(_BF16), wglu_ref[...], preferred_element_type=_F32) + bglu_ref[...]
    ssm = y * jax.nn.sigmoid(gate)
    conv = jnp.dot(c_ref[...], wpw_ref[...], preferred_element_type=_F32) + bpw_ref[...]
    mixed = jnp.dot(a_ref[...], wo_ref[0:aw, :], preferred_element_type=_F32)
    mixed += jnp.dot(ssm.astype(_BF16), wo_ref[aw:aw + sw, :], preferred_element_type=_F32)
    mixed += jnp.dot(conv.astype(_BF16), wo_ref[aw + sw:, :], preferred_element_type=_F32)
    o_ref[...] = h_ref[...] + mixed


def _outproj(h, attn, y, conv, wglu, bglu, wpw, bpw, wo):
    n, d = h.shape
    aw, sw, cw = attn.shape[1], y.shape[1], conv.shape[1]
    tm = _pick_tile(n, 640, SUBLANES)
    row = lambda width: pl.BlockSpec((tm, width), lambda i: (i, 0))
    return pl.pallas_call(
        functools.partial(_outproj_kernel, aw=aw, sw=sw),
        out_shape=jax.ShapeDtypeStruct((n, d), _F32),
        grid=(n // tm,),
        in_specs=[row(d), row(aw), row(sw), row(cw),
                  _resident(wglu.shape), _resident((1, sw)),
                  _resident(wpw.shape), _resident((1, cw)), _resident(wo.shape)],
        out_specs=row(d),
        compiler_params=_params("parallel"),
        name="outproj",
    )(h, attn, y, conv, wglu, bglu, wpw, bpw, wo)


def kernel(x, meta, ffn1_norm, ffn1_w_gate, ffn1_w_up, ffn1_w_down, mix_norm, w_in, q_norm, k_norm, ssm_lambda_re, ssm_lambda_im, ssm_log_step, ssm_b_re, ssm_b_im, ssm_c_re, ssm_c_im, ssm_d, ssm_w_glu, ssm_b_glu, conv_w_dw, conv_b_dw, conv_ln_g, conv_ln_b, conv_w_pw, conv_b_pw, w_out, ffn2_norm, ffn2_w_gate, ffn2_w_up, ffn2_w_down, post_norm):
    batch, seq, d = x.shape
    depth = ffn1_norm.shape[0]
    n_meta = meta.shape[0]
    length = seq + n_meta
    sw = ssm_w_glu.shape[1]
    cw = conv_w_pw.shape[1]
    aw = (w_in.shape[2] - sw - 2 * cw) // 3
    heads = aw // HEAD_DIM
    assert aw % LANES == 0 and sw % LANES == 0 and LANES % SSM_GROUP == 0

    p = -(-length // ATT_BLOCK) * ATT_BLOCK
    n = batch * p

    head_sum = jnp.kron(jnp.eye(heads, dtype=_F32), jnp.ones((HEAD_DIM, HEAD_DIM), _F32)).astype(_BF16)
    ids = jnp.arange(ATT_BLOCK)
    tri = jnp.concatenate([(ids[:, None] > ids[None, :]).astype(_BF16),
                           jnp.ones((ATT_BLOCK, ATT_BLOCK), _BF16)], axis=1)
    logit_scale = HEAD_DIM ** -0.5 * math.log2(math.e)
    vec = lambda a: a.astype(_F32).reshape(1, -1)

    for l in range(depth):
        first, last = l == 0, l == depth - 1
        h = _ffn((x, meta.astype(x.dtype)) if first else (h,), vec(ffn1_norm[l]),
                 ffn1_w_gate[l].astype(_BF16), ffn1_w_up[l].astype(_BF16),
                 ffn1_w_down[l].astype(_BF16),
                 source="embed" if first else "rows", p=p, n_meta=n_meta)
        h = h.reshape(n, d)

        q, k, v, u, ch = _inproj(
            h, vec(mix_norm[l]), w_in[l].astype(_BF16),
            vec(jnp.tile(q_norm[l].astype(_F32) * logit_scale, heads)),
            vec(jnp.tile(k_norm[l].astype(_F32), heads)), head_sum, aw, sw, cw)

        attn = _attention(q.reshape(batch, p, aw), k.reshape(batch, p, aw),
                          v.reshape(batch, p, aw), tri)
        mats = _ssm_matrices(ssm_lambda_re[l], ssm_lambda_im[l], ssm_log_step[l],
                             ssm_b_re[l], ssm_b_im[l], ssm_c_re[l], ssm_c_im[l], ssm_d[l])
        y = _ssm(u.reshape(batch, p, sw), mats)
        conv = _conv(ch.reshape(batch, p, cw), conv_w_dw[l].astype(_F32),
                     vec(conv_b_dw[l]), vec(conv_ln_g[l]), vec(conv_ln_b[l]))

        h = _outproj(h, attn.reshape(n, aw), y.reshape(n, sw), conv.reshape(n, cw),
                     ssm_w_glu[l].astype(_BF16), vec(ssm_b_glu[l]),
                     conv_w_pw[l].astype(_BF16), vec(conv_b_pw[l]),
                     w_out[l].astype(_BF16))

        h = _ffn((h.reshape(batch, p, d),), vec(ffn2_norm[l]),
                 ffn2_w_gate[l].astype(_BF16), ffn2_w_up[l].astype(_BF16),
                 ffn2_w_down[l].astype(_BF16), post_g=vec(post_norm[l]),
                 source="shifted" if last else "rows", p=p, seq=seq, n_meta=n_meta)

    return h
```

```python
import functools
import math

import jax
import jax.numpy as jnp
from jax import lax
from jax.experimental import pallas as pl
from jax.experimental.pallas import tpu as pltpu

_F32 = jnp.float32
_BF16 = jnp.bfloat16

HEAD_DIM = 64
SSM_GROUP = 16
DW_TAPS_HALO = 32
RMS_EPS = 1e-6
LN_EPS = 1e-5
LAMBDA_RE_MAX = -1e-4
LANES = 128
SUBLANES = 8
SSM_CHUNK = SUBLANES
ATT_BLOCK = 128
ATT_LOG2_MASS_FLOOR = -40.0 * math.log2(math.e)
VMEM_LIMIT_BYTES = 56 * 1024 * 1024
CAST_BLOCK_BYTES = 6 * 1024 * 1024


def _pick_tile(n, target, mult):
    best = None
    for t in range(mult, min(n, target) + 1, mult):
        if n % t == 0:
            best = t
    assert best is not None, (n, target, mult)
    return best


def _params(*sem):
    return pltpu.CompilerParams(dimension_semantics=sem,
                                vmem_limit_bytes=VMEM_LIMIT_BYTES)


def _rms(x, g):
    ms = jnp.mean(x * x, axis=-1, keepdims=True)
    return x * lax.rsqrt(ms + RMS_EPS) * g


def _resident(shape):
    zeros = (0,) * len(shape)
    return pl.BlockSpec(shape, lambda *_: zeros, pipeline_mode=pl.Buffered(1))


def _cast_kernel(w_ref, o_ref):
    o_ref[...] = w_ref[...].astype(o_ref.dtype)


def _layer_bf16(w, l):
    _, r, c = w.shape
    pack = 2 * SUBLANES
    tr = _pick_tile(r, max(pack, CAST_BLOCK_BYTES // (c * w.dtype.itemsize)), pack)
    return pl.pallas_call(
        _cast_kernel,
        out_shape=jax.ShapeDtypeStruct((r, c), _BF16),
        grid=(r // tr,),
        in_specs=[pl.BlockSpec((None, tr, c), lambda i: (l, i, 0))],
        out_specs=pl.BlockSpec((tr, c), lambda i: (i, 0)),
        compiler_params=_params("parallel"),
        name="cast_bf16",
    )(w)


_FFN_SOURCE_REFS = {"rows": 1, "embed": 3, "shifted": 2}


def _ffn_kernel(*refs, source, post, nf, n_meta, length):
    n_src = _FFN_SOURCE_REFS[source]
    src = refs[:n_src]
    g_ref, wg_ref, wu_ref, wd_ref = refs[n_src:n_src + 4]
    pg_ref = refs[n_src + 4] if post else None
    o_ref, xn_ref = refs[-2:]
    i = pl.program_id(1)
    f = pl.program_id(2)
    tm = o_ref.shape[0]

    @pl.when(f == 0)
    def _():
        if source == "rows":
            x = src[0][...]
        elif source == "embed":
            tail_ref, body_ref, meta_ref = src
            head = jnp.where(i == 0, meta_ref[...], tail_ref[...])
            x = jnp.concatenate([head, body_ref[0:tm - n_meta, :]], axis=0)
            row = i * tm + lax.broadcasted_iota(jnp.int32, (tm, 1), 0)
            x = jnp.where(row < length, x, 0.0)
        else:
            body_ref, next_ref = src
            x = jnp.concatenate([body_ref[n_meta:, :], next_ref[...]], axis=0)
        xn_ref[...] = _rms(x, g_ref[...]).astype(_BF16)
        o_ref[...] = x

    xn = xn_ref[...]
    gate = jnp.dot(xn, wg_ref[...], preferred_element_type=_F32)
    up = jnp.dot(xn, wu_ref[...], preferred_element_type=_F32)
    act = (gate * jax.nn.sigmoid(gate)) * (up * 0.5)
    o_ref[...] += jnp.dot(act.astype(_BF16), wd_ref[...],
                          preferred_element_type=_F32)

    if post:
        @pl.when(f == nf - 1)
        def _():
            o_ref[...] = _rms(o_ref[...], pg_ref[...])


def _ffn(src, g, wg, wu, wd, post_g=None, source="rows", p=None, seq=None, n_meta=SUBLANES):
    b, rows_in, d = src[0].shape
    p = rows_in if p is None else p
    dff = wg.shape[1]
    assert n_meta % SUBLANES == 0
    tm = _pick_tile(p, 640, n_meta)
    tf = _pick_tile(dff, 512, LANES)
    per_tile = tm // n_meta
    post = post_g is not None
    tile = pl.BlockSpec((None, tm, d), lambda bi, i, j: (bi, i, 0))
    vec = pl.BlockSpec((1, d), lambda bi, i, j: (0, 0))
    length = 0
    if source == "rows":
        out_rows = p
        src_specs, args = [tile], [src[0]]
    elif source == "embed":
        x, meta = src
        length = rows_in + n_meta
        out_rows = p
        last = pl.cdiv(rows_in, tm) - 1
        src_specs = [
            pl.BlockSpec((None, n_meta, d),
                         lambda bi, i, j: (bi, jnp.maximum(i * per_tile - 1, 0), 0)),
            pl.BlockSpec((None, tm, d), lambda bi, i, j: (bi, jnp.minimum(i, last), 0)),
            pl.BlockSpec((n_meta, d), lambda bi, i, j: (0, 0))]
        args = [x, x, meta]
    else:
        out_rows = seq
        last = p // n_meta - 1
        src_specs = [
            tile,
            pl.BlockSpec((None, n_meta, d),
                         lambda bi, i, j: (bi, jnp.minimum((i + 1) * per_tile, last), 0))]
        args = [src[0], src[0]]
    in_specs = src_specs + [vec,
                            pl.BlockSpec((d, tf), lambda bi, i, j: (0, j)),
                            pl.BlockSpec((d, tf), lambda bi, i, j: (0, j)),
                            pl.BlockSpec((tf, d), lambda bi, i, j: (j, 0))]
    args += [g, wg, wu, wd]
    if post:
        in_specs.append(vec)
        args.append(post_g)
    return pl.pallas_call(
        functools.partial(_ffn_kernel, source=source, post=post, nf=dff // tf,
                          n_meta=n_meta, length=length),
        out_shape=jax.ShapeDtypeStruct((b, out_rows, d), _F32),
        grid=(b, pl.cdiv(out_rows, tm), dff // tf),
        in_specs=in_specs,
        out_specs=tile,
        scratch_shapes=[pltpu.VMEM((tm, d), _BF16)],
        compiler_params=_params("parallel", "parallel", "arbitrary"),
        name="ffn_" + source + ("_post" if post else ""),
    )(*args)


def _inproj_kernel(h_ref, g_ref, w_ref, qg_ref, kg_ref, hsum_ref,
                   q_ref, k_ref, v_ref, u_ref, c_ref, *, aw, sw, cw):
    xn = _rms(h_ref[...], g_ref[...]).astype(_BF16)

    def proj(lo, width):
        return jnp.dot(xn, w_ref[:, lo:lo + width], preferred_element_type=_F32)

    def head_norm(t, gain):
        ss = jnp.dot((t * t).astype(_BF16), hsum_ref[...],
                     preferred_element_type=_F32)
        return t * lax.rsqrt(ss * (1.0 / HEAD_DIM) + RMS_EPS) * gain

    q_ref[...] = head_norm(proj(0, aw), qg_ref[...]).astype(_BF16)
    k_ref[...] = head_norm(proj(aw, aw), kg_ref[...]).astype(_BF16)
    v_ref[...] = proj(2 * aw, aw).astype(_BF16)
    u_ref[...] = proj(3 * aw, sw)
    a = proj(3 * aw + sw, cw)
    gt = proj(3 * aw + sw + cw, cw)
    c_ref[...] = a * jax.nn.sigmoid(gt)


def _inproj(h, g, w, qg, kg, hsum, aw, sw, cw):
    n, d = h.shape
    tm = _pick_tile(n, 640, SUBLANES)
    row = lambda width: pl.BlockSpec((tm, width), lambda i: (i, 0))
    return pl.pallas_call(
        functools.partial(_inproj_kernel, aw=aw, sw=sw, cw=cw),
        out_shape=(jax.ShapeDtypeStruct((n, aw), _BF16),
                   jax.ShapeDtypeStruct((n, aw), _BF16),
                   jax.ShapeDtypeStruct((n, aw), _BF16),
                   jax.ShapeDtypeStruct((n, sw), _F32),
                   jax.ShapeDtypeStruct((n, cw), _F32)),
        grid=(n // tm,),
        in_specs=[row(d), _resident((1, d)), _resident(w.shape),
                  _resident((1, aw)), _resident((1, aw)), _resident((aw, aw))],
        out_specs=(row(aw), row(aw), row(aw), row(sw), row(cw)),
        compiler_params=_params("parallel"),
        name="inproj",
    )(h, g, w, qg, kg, hsum)


def _attn_kernel(q_ref, k_ref, v_ref, tri_ref, o_ref, q2_ref, acc_ref, run_ref):
    tq = ATT_BLOCK
    npairs = q_ref.shape[1] // LANES
    qi = pl.program_id(1)
    first = lax.broadcasted_iota(jnp.int32, (tq, LANES), 1) < HEAD_DIM
    row_id = lax.broadcasted_iota(jnp.int32, (2 * tq, tq), 0)
    col_id = lax.broadcasted_iota(jnp.int32, (2 * tq, tq), 1)
    earlier = col_id < jnp.where(row_id >= tq, row_id - tq, row_id)

    for p in range(npairs):
        q = q_ref[:, p * LANES:(p + 1) * LANES]
        zero = jnp.zeros_like(q)
        q2_ref[p] = jnp.concatenate(
            [jnp.where(first, q, zero), jnp.where(first, zero, q)], axis=0)
    acc_ref[...] = jnp.zeros_like(acc_ref)
    run_ref[...] = jnp.zeros_like(run_ref)

    def sweep(j, masked):
        start = pl.multiple_of(j * tq, tq)
        keys = pl.ds(start, tq)
        pair_lanes = [slice(p * LANES, (p + 1) * LANES) for p in range(npairs)]
        tri = tri_ref[...]

        logits = [lax.dot_general(q2_ref[p], k_ref[keys, pair_lanes[p]],
                                  (((1,), (1,)), ((), ())),
                                  preferred_element_type=_F32)
                  for p in range(npairs)]

        partial = []
        for z in logits:
            nz = -z
            log_keep = jnp.minimum(nz, 0.0) - jnp.log2(1.0 + jnp.exp2(jnp.minimum(z, nz)))
            if masked:
                log_keep = jnp.where(earlier, log_keep, 0.0)
            hi = log_keep.astype(_BF16)
            lo = (log_keep - hi.astype(_F32)).astype(_BF16)
            sums = (jnp.dot(hi, tri, preferred_element_type=_F32)
                    + jnp.dot(lo, tri, preferred_element_type=_F32))
            partial.append((log_keep + z, sums))

        mass = None
        for p, (log_hit, sums) in enumerate(partial):
            run = run_ref[p]
            att = jnp.exp2(log_hit + sums[:, :tq] + run)
            if masked:
                att = jnp.where(earlier, att, 0.0)
            att2 = jnp.concatenate([att[:tq], att[tq:]], axis=1).astype(_BF16)
            vb = v_ref[keys, pair_lanes[p]]
            vzero = jnp.zeros_like(vb)
            v2 = jnp.concatenate(
                [jnp.where(first, vb, vzero), jnp.where(first, vzero, vb)], axis=0)
            acc_ref[:, pair_lanes[p]] += jnp.dot(att2, v2, preferred_element_type=_F32)
            run = run + sums[:, tq:]
            run_ref[p] = run
            mass = run if mass is None else jnp.maximum(mass, run)
        return jnp.max(mass)

    sweep(qi, True)

    def more(state):
        j, live = state
        return jnp.logical_and(j >= 0, live)

    def step(state):
        j, _ = state
        return j - 1, sweep(j, False) > ATT_LOG2_MASS_FLOOR

    lax.while_loop(more, step, (qi - 1, qi >= 0))
    o_ref[...] = acc_ref[...].astype(o_ref.dtype)


def _attention(q, k, v, tri):
    b, p, aw = q.shape
    tq = ATT_BLOCK
    npairs = aw // LANES
    blk = pl.BlockSpec((None, tq, aw), lambda bi, i: (bi, i, 0))
    full = pl.BlockSpec((None, p, aw), lambda bi, i: (bi, 0, 0))
    return pl.pallas_call(
        _attn_kernel,
        out_shape=jax.ShapeDtypeStruct((b, p, aw), _BF16),
        grid=(b, p // tq),
        in_specs=[blk, full, full, _resident(tri.shape)],
        out_specs=blk,
        scratch_shapes=[pltpu.VMEM((npairs, 2 * tq, LANES), _BF16),
                        pltpu.VMEM((tq, aw), _F32),
                        pltpu.VMEM((npairs, 2 * tq, LANES), _F32)],
        compiler_params=_params("parallel", "arbitrary"),
        name="stickbreak_attn",
    )(q, k, v, tri)


def _ssm_kernel(u_ref, kmat_ref, bmat_ref, emat_ref, apow_ref, d_ref, y_ref,
                uc_ref, s_ref):
    t = SSM_CHUNK
    nc = uc_ref.shape[0]
    half = s_ref.shape[1] // 2

    for k in range(t):
        uc_ref[:, k * LANES:(k + 1) * LANES] = u_ref[pl.ds(k, nc, stride=t), :].astype(_BF16)
    s_ref[...] = jnp.dot(uc_ref[...], bmat_ref[...], preferred_element_type=_F32)

    pw_re, pw_im = apow_ref[0], apow_ref[1]
    sub = lax.broadcasted_iota(jnp.int32, (SUBLANES, half), 0)

    def cmul(ar, ai, xr, xi):
        return ar * xr - ai * xi, ar * xi + ai * xr

    def shifted(x, k):
        return jnp.where(sub >= k, pltpu.roll(x, k, 0), 0.0)

    def bcast_row(x, r):
        return jnp.broadcast_to(x[r:r + 1, :], x.shape)

    levels = []
    for k in (1, 2, 4):
        levels.append((k, bcast_row(pw_re, k - 1), bcast_row(pw_im, k - 1)))

    def step(i, carry):
        in_re, in_im = carry
        r0 = pl.multiple_of(i * SUBLANES, SUBLANES)
        p_re = s_ref[pl.ds(r0, SUBLANES), 0:half]
        p_im = s_ref[pl.ds(r0, SUBLANES), half:2 * half]
        for k, ar, ai in levels:
            d_re, d_im = cmul(ar, ai, shifted(p_re, k), shifted(p_im, k))
            p_re, p_im = p_re + d_re, p_im + d_im
        c_re, c_im = cmul(pw_re, pw_im, in_re, in_im)
        out_re, out_im = p_re + c_re, p_im + c_im
        ent_re = jnp.where(sub >= 1, pltpu.roll(out_re, 1, 0), in_re)
        ent_im = jnp.where(sub >= 1, pltpu.roll(out_im, 1, 0), in_im)
        s_ref[pl.ds(r0, SUBLANES), 0:half] = ent_re
        s_ref[pl.ds(r0, SUBLANES), half:2 * half] = ent_im
        return bcast_row(out_re, SUBLANES - 1), bcast_row(out_im, SUBLANES - 1)

    zero = jnp.zeros((SUBLANES, half), _F32)
    lax.fori_loop(0, nc // SUBLANES, step, (zero, zero))

    y = jnp.dot(uc_ref[...], kmat_ref[...], preferred_element_type=_F32)
    y += jnp.dot(s_ref[...].astype(_BF16), emat_ref[...], preferred_element_type=_F32)
    for k in range(t):
        rows = pl.ds(k, nc, stride=t)
        y_ref[rows, :] = y[:, k * LANES:(k + 1) * LANES] + d_ref[...] * u_ref[rows, :]


def _ssm_matrices(lam_re, lam_im, log_step, b_re, b_im, c_re, c_im, d_skip):
    t = SSM_CHUNK
    g, p = lam_re.shape
    hg = b_re.shape[-1]
    gl = LANES // hg
    nb = g // gl
    hp = lax.Precision.HIGHEST
    lam = lax.complex(jnp.minimum(lam_re.astype(_F32), LAMBDA_RE_MAX), lam_im.astype(_F32))
    lam_dt = lam * jnp.exp(log_step.astype(_F32))[:, None]
    lam_bar = jnp.exp(lam_dt)
    b_bar = ((lam_bar - 1.0) / lam)[:, :, None] * lax.complex(b_re.astype(_F32), b_im.astype(_F32))
    c_cplx = lax.complex(c_re.astype(_F32), c_im.astype(_F32))
    steps = jnp.arange(t, dtype=_F32)
    powers = jnp.exp(lam_dt[None] * steps[:, None, None])
    eye = jnp.eye(gl, dtype=_F32)

    ktau = jnp.real(jnp.einsum('gop,tgp,gpi->gtio', c_cplx, powers, b_bar, precision=hp))
    kblk = jnp.einsum('jgtio,gk->jtgiko', ktau.reshape(nb, gl, t, hg, hg), eye)
    kblk = kblk.reshape(nb, t, LANES, LANES).astype(_BF16)
    lag = jnp.arange(t)[None, :] - jnp.arange(t)[:, None]
    kbig = jnp.where((lag >= 0)[None, :, :, None, None],
                     kblk[:, jnp.clip(lag, 0, t - 1)], 0.0)
    kmat = kbig.transpose(0, 1, 3, 2, 4).reshape(nb, t * LANES, t * LANES)

    bfull = (powers[::-1].transpose(1, 0, 2)[:, :, None, :]
             * b_bar.transpose(0, 2, 1)[:, None, :, :])
    bfull = bfull.reshape(nb, gl, t, hg, p)

    def spread_b(m):
        return jnp.einsum('jgthp,gk->jtghkp', m, eye).reshape(nb, t * LANES, gl * p)
    bmat = jnp.concatenate([spread_b(jnp.real(bfull)), spread_b(jnp.imag(bfull))], axis=2)

    efull = (c_cplx.transpose(0, 2, 1)[:, :, None, :]
             * (powers * lam_bar[None]).transpose(1, 2, 0)[:, :, :, None])
    efull = efull.reshape(nb, gl, p, t, hg)

    def spread_e(m):
        return jnp.einsum('jgpth,gk->jgptkh', m, eye).reshape(nb, gl * p, t * LANES)
    emat = jnp.concatenate([spread_e(jnp.real(efull)), spread_e(-jnp.imag(efull))], axis=1)

    rows = jnp.arange(1, SUBLANES + 1, dtype=_F32) * t
    apow = jnp.exp(lam_dt[None] * rows[:, None, None])
    apow = apow.reshape(SUBLANES, nb, gl * p).transpose(1, 0, 2)
    apow = jnp.stack([jnp.real(apow), jnp.imag(apow)], axis=1)
    return (kmat, bmat.astype(_BF16), emat.astype(_BF16), apow,
            d_skip.astype(_F32).reshape(nb, 1, LANES))


def _ssm(u, mats):
    b, p, sw = u.shape
    t = SSM_CHUNK
    nc = p // t
    assert nc % SUBLANES == 0 and sw % LANES == 0
    kmat, bmat, emat, apow, dvec = mats
    nb, lw, sl = bmat.shape
    col = pl.BlockSpec((None, p, LANES), lambda j, bi: (bi, 0, j))
    per = lambda *shape: pl.BlockSpec((None,) + shape, lambda j, bi: (j,) + (0,) * len(shape))
    return pl.pallas_call(
        _ssm_kernel,
        out_shape=jax.ShapeDtypeStruct(u.shape, _F32),
        grid=(nb, b),
        in_specs=[col, per(lw, lw), per(lw, sl), per(sl, lw),
                  per(2, SUBLANES, sl // 2), per(1, LANES)],
        out_specs=col,
        scratch_shapes=[pltpu.VMEM((nc, lw), _BF16), pltpu.VMEM((nc, sl), _F32)],
        compiler_params=_params("parallel", "parallel"),
        name="s5_ssm",
    )(u, kmat, bmat, emat, apow, dvec)


def _conv_kernel(halo_ref, x_ref, w_ref, b_ref, lg_ref, lb_ref, o_ref,
                 sh_ref, acc_ref, wb_ref, *, taps):
    tt, c = x_ref.shape
    halo = DW_TAPS_HALO
    lead = halo - (taps - 1)
    first_tile = pl.program_id(1) == 0
    sh_ref[0, 0:halo, :] = jnp.where(first_tile, 0.0, halo_ref[...])
    sh_ref[0, halo:halo + tt, :] = x_ref[...]
    body_rows = tt + halo - SUBLANES
    for s in range(1, SUBLANES):
        sh_ref[s, 0:body_rows, :] = sh_ref[0, s:s + body_rows, :]

    for k in range(taps):
        wb_ref[k] = jnp.broadcast_to(w_ref[k:k + 1, :], (SUBLANES, c))
    bias = jnp.broadcast_to(b_ref[...], (SUBLANES, c))
    halves = 4
    rows = halves * SUBLANES

    def tile(r, _):
        r0 = pl.multiple_of(r * rows, rows)
        acc = [bias] * halves
        for k in range(taps):
            a, s = divmod(lead + k, SUBLANES)
            w = wb_ref[k]
            for i in range(halves):
                acc[i] = acc[i] + w * sh_ref[s, pl.ds(r0 + (a + i) * SUBLANES, SUBLANES), :]
        for i in range(halves):
            acc_ref[pl.ds(r0 + i * SUBLANES, SUBLANES), :] = acc[i]
        return 0

    lax.fori_loop(0, tt // rows, tile, 0)

    y = acc_ref[...]
    yc = y - jnp.mean(y, axis=-1, keepdims=True)
    yn = yc * lax.rsqrt(jnp.mean(yc * yc, axis=-1, keepdims=True) + LN_EPS)
    yn = yn * lg_ref[...] + lb_ref[...]
    o_ref[...] = (yn * jax.nn.sigmoid(yn)).astype(o_ref.dtype)


def _conv(x, w, bias, ln_g, ln_b):
    b, p, c = x.shape
    taps = w.shape[0]
    halo = DW_TAPS_HALO
    assert taps - 1 <= halo
    tt = _pick_tile(p, 640, halo)
    per_tile = tt // halo
    vec = lambda r: pl.BlockSpec((r, c), lambda bi, i: (0, 0))
    return pl.pallas_call(
        functools.partial(_conv_kernel, taps=taps),
        out_shape=jax.ShapeDtypeStruct((b, p, c), _BF16),
        grid=(b, p // tt),
        in_specs=[pl.BlockSpec((None, halo, c),
                               lambda bi, i: (bi, jnp.maximum(i * per_tile - 1, 0), 0)),
                  pl.BlockSpec((None, tt, c), lambda bi, i: (bi, i, 0)),
                  vec(taps), vec(1), vec(1), vec(1)],
        out_specs=pl.BlockSpec((None, tt, c), lambda bi, i: (bi, i, 0)),
        scratch_shapes=[pltpu.VMEM((SUBLANES, tt + halo, c), _F32),
                        pltpu.VMEM((tt, c), _F32),
                        pltpu.VMEM((taps, SUBLANES, c), _F32)],
        compiler_params=_params("parallel", "arbitrary"),
        name="dwconv_ln_swish",
    )(x, x, w, bias, ln_g, ln_b)


def _gelu_tanh(x):
    c = 0.7978845608028654
    return 0.5 * x * (1.0 + jnp.tanh(c * (x + 0.044715 * (x * x * x))))


def _outproj_kernel(h_ref, a_ref, y_ref, c_ref, wglu_ref, bglu_ref, wpw_ref,
                    bpw_ref, wo_ref, o_ref, *, aw, sw):
    y = _gelu_tanh(y_ref[...])
    gate = jnp.dot(y.astype(_BF16), wglu_ref[...], preferred_element_type=_F32) + bglu_ref[...]
    ssm = y * jax.nn.sigmoid(gate)
    conv = jnp.dot(c_ref[...], wpw_ref[...], preferred_element_type=_F32) + bpw_ref[...]
    mixed = jnp.dot(a_ref[...], wo_ref[0:aw, :], preferred_element_type=_F32)
    mixed += jnp.dot(ssm.astype(_BF16), wo_ref[aw:aw + sw, :], preferred_element_type=_F32)
    mixed += jnp.dot(conv.astype(_BF16), wo_ref[aw + sw:, :], preferred_element_type=_F32)
    o_ref[...] = h_ref[...] + mixed


def _outproj(h, attn, y, conv, wglu, bglu, wpw, bpw, wo):
    n, d = h.shape
    aw, sw, cw = attn.shape[1], y.shape[1], conv.shape[1]
    tm = _pick_tile(n, 640, SUBLANES)
    row = lambda width: pl.BlockSpec((tm, width), lambda i: (i, 0))
    return pl.pallas_call(
        functools.partial(_outproj_kernel, aw=aw, sw=sw),
        out_shape=jax.ShapeDtypeStruct((n, d), _F32),
        grid=(n // tm,),
        in_specs=[row(d), row(aw), row(sw), row(cw),
                  _resident(wglu.shape), _resident((1, sw)),
                  _resident(wpw.shape), _resident((1, cw)), _resident(wo.shape)],
        out_specs=row(d),
        compiler_params=_params("parallel"),
        name="outproj",
    )(h, attn, y, conv, wglu, bglu, wpw, bpw, wo)


def kernel(x, meta, ffn1_norm, ffn1_w_gate, ffn1_w_up, ffn1_w_down, mix_norm, w_in, q_norm, k_norm, ssm_lambda_re, ssm_lambda_im, ssm_log_step, ssm_b_re, ssm_b_im, ssm_c_re, ssm_c_im, ssm_d, ssm_w_glu, ssm_b_glu, conv_w_dw, conv_b_dw, conv_ln_g, conv_ln_b, conv_w_pw, conv_b_pw, w_out, ffn2_norm, ffn2_w_gate, ffn2_w_up, ffn2_w_down, post_norm):
    batch, seq, d = x.shape
    depth = ffn1_norm.shape[0]
    n_meta = meta.shape[0]
    length = seq + n_meta
    sw = ssm_w_glu.shape[1]
    cw = conv_w_pw.shape[1]
    aw = (w_in.shape[2] - sw - 2 * cw) // 3
    heads = aw // HEAD_DIM
    assert aw % LANES == 0 and sw % LANES == 0 and LANES % SSM_GROUP == 0

    p = -(-length // ATT_BLOCK) * ATT_BLOCK
    n = batch * p

    head_sum = jnp.kron(jnp.eye(heads, dtype=_F32), jnp.ones((HEAD_DIM, HEAD_DIM), _F32)).astype(_BF16)
    ids = jnp.arange(ATT_BLOCK)
    tri = jnp.concatenate([(ids[:, None] > ids[None, :]).astype(_BF16),
                           jnp.ones((ATT_BLOCK, ATT_BLOCK), _BF16)], axis=1)
    logit_scale = HEAD_DIM ** -0.5 * math.log2(math.e)
    vec = lambda a: a.astype(_F32).reshape(1, -1)

    for l in range(depth):
        first, last = l == 0, l == depth - 1
        h = _ffn((x, meta.astype(x.dtype)) if first else (h,), vec(ffn1_norm[l]),
                 _layer_bf16(ffn1_w_gate, l), _layer_bf16(ffn1_w_up, l),
                 _layer_bf16(ffn1_w_down, l),
                 source="embed" if first else "rows", p=p, n_meta=n_meta)
        h = h.reshape(n, d)

        q, k, v, u, ch = _inproj(
            h, vec(mix_norm[l]), _layer_bf16(w_in, l),
            vec(jnp.tile(q_norm[l].astype(_F32) * logit_scale, heads)),
            vec(jnp.tile(k_norm[l].astype(_F32), heads)), head_sum, aw, sw, cw)

        attn = _attention(q.reshape(batch, p, aw), k.reshape(batch, p, aw),
                          v.reshape(batch, p, aw), tri)
        mats = _ssm_matrices(ssm_lambda_re[l], ssm_lambda_im[l], ssm_log_step[l],
                             ssm_b_re[l], ssm_b_im[l], ssm_c_re[l], ssm_c_im[l], ssm_d[l])
        y = _ssm(u.reshape(batch, p, sw), mats)
        conv = _conv(ch.reshape(batch, p, cw), conv_w_dw[l].astype(_F32),
                     vec(conv_b_dw[l]), vec(conv_ln_g[l]), vec(conv_ln_b[l]))

        h = _outproj(h, attn.reshape(n, aw), y.reshape(n, sw), conv.reshape(n, cw),
                     _layer_bf16(ssm_w_glu, l), vec(ssm_b_glu[l]),
                     _layer_bf16(conv_w_pw, l), vec(conv_b_pw[l]),
                     _layer_bf16(w_out, l))

        h = _ffn((h.reshape(batch, p, d),), vec(ffn2_norm[l]),
                 _layer_bf16(ffn2_w_gate, l), _layer_bf16(ffn2_w_up, l),
                 _layer_bf16(ffn2_w_down, l), post_g=vec(post_norm[l]),
                 source="shifted" if last else "rows", p=p, seq=seq, n_meta=n_meta)

    return h
```

```python
import functools
import math

import jax
import jax.numpy as jnp
from jax import lax
from jax.experimental import pallas as pl
from jax.experimental.pallas import tpu as pltpu

_F32 = jnp.float32
_BF16 = jnp.bfloat16

HEAD_DIM = 64
SSM_GROUP = 16
DW_TAPS_HALO = 32
RMS_EPS = 1e-6
LN_EPS = 1e-5
LAMBDA_RE_MAX = -1e-4
LANES = 128
SUBLANES = 8
MXU_WIDTH = 256
SSM_CHUNK = SUBLANES
ATT_BLOCK = 128
ATT_LOG2_MASS_FLOOR = -40.0 * math.log2(math.e)
VMEM_LIMIT_BYTES = 56 * 1024 * 1024
CAST_BLOCK_BYTES = 6 * 1024 * 1024


def _pick_tile(n, target, mult):
    best = None
    for t in range(mult, min(n, target) + 1, mult):
        if n % t == 0:
            best = t
    assert best is not None, (n, target, mult)
    return best


def _params(*sem):
    return pltpu.CompilerParams(dimension_semantics=sem,
                                vmem_limit_bytes=VMEM_LIMIT_BYTES)


def _rms(x, g):
    ms = jnp.mean(x * x, axis=-1, keepdims=True)
    return x * lax.rsqrt(ms + RMS_EPS) * g


def _resident(shape):
    zeros = (0,) * len(shape)
    return pl.BlockSpec(shape, lambda *_: zeros, pipeline_mode=pl.Buffered(1))


def _cast_kernel(w_ref, o_ref):
    o_ref[...] = w_ref[...].astype(o_ref.dtype)


def _layer_bf16(w, l):
    _, r, c = w.shape
    pack = 2 * SUBLANES
    tr = _pick_tile(r, max(pack, CAST_BLOCK_BYTES // (c * w.dtype.itemsize)), pack)
    return pl.pallas_call(
        _cast_kernel,
        out_shape=jax.ShapeDtypeStruct((r, c), _BF16),
        grid=(r // tr,),
        in_specs=[pl.BlockSpec((None, tr, c), lambda i: (l, i, 0))],
        out_specs=pl.BlockSpec((tr, c), lambda i: (i, 0)),
        compiler_params=_params("parallel"),
        name="cast_bf16",
    )(w)


_FFN_SOURCE_REFS = {"rows": 1, "embed": 3, "shifted": 2}


def _ffn_kernel(*refs, source, post, nf, n_meta, length):
    n_src = _FFN_SOURCE_REFS[source]
    src = refs[:n_src]
    g_ref, wg_ref, wu_ref, wd_ref = refs[n_src:n_src + 4]
    pg_ref = refs[n_src + 4] if post else None
    o_ref, xn_ref = refs[-2:]
    i = pl.program_id(1)
    f = pl.program_id(2)
    tm = o_ref.shape[0]

    @pl.when(f == 0)
    def _():
        if source == "rows":
            x = src[0][...]
        elif source == "embed":
            tail_ref, body_ref, meta_ref = src
            head = jnp.where(i == 0, meta_ref[...], tail_ref[...])
            x = jnp.concatenate([head, body_ref[0:tm - n_meta, :]], axis=0)
            row = i * tm + lax.broadcasted_iota(jnp.int32, (tm, 1), 0)
            x = jnp.where(row < length, x, 0.0)
        else:
            body_ref, next_ref = src
            x = jnp.concatenate([body_ref[n_meta:, :], next_ref[...]], axis=0)
        xn_ref[...] = _rms(x, g_ref[...]).astype(_BF16)
        o_ref[...] = x

    xn = xn_ref[...]
    gate = jnp.dot(xn, wg_ref[...], preferred_element_type=_F32)
    up = jnp.dot(xn, wu_ref[...], preferred_element_type=_F32)
    act = (gate * jax.nn.sigmoid(gate)) * (up * 0.5)
    o_ref[...] += jnp.dot(act.astype(_BF16), wd_ref[...],
                          preferred_element_type=_F32)

    if post:
        @pl.when(f == nf - 1)
        def _():
            o_ref[...] = _rms(o_ref[...], pg_ref[...])


def _ffn(src, g, wg, wu, wd, post_g=None, source="rows", p=None, seq=None, n_meta=SUBLANES):
    b, rows_in, d = src[0].shape
    p = rows_in if p is None else p
    dff = wg.shape[1]
    assert n_meta % SUBLANES == 0
    tm = _pick_tile(p, 832, n_meta)
    tf = _pick_tile(dff, 512, LANES)
    per_tile = tm // n_meta
    post = post_g is not None
    tile = pl.BlockSpec((None, tm, d), lambda bi, i, j: (bi, i, 0))
    vec = pl.BlockSpec((1, d), lambda bi, i, j: (0, 0))
    length = 0
    if source == "rows":
        out_rows = p
        src_specs, args = [tile], [src[0]]
    elif source == "embed":
        x, meta = src
        length = rows_in + n_meta
        out_rows = p
        last = pl.cdiv(rows_in, tm) - 1
        src_specs = [
            pl.BlockSpec((None, n_meta, d),
                         lambda bi, i, j: (bi, jnp.maximum(i * per_tile - 1, 0), 0)),
            pl.BlockSpec((None, tm, d), lambda bi, i, j: (bi, jnp.minimum(i, last), 0)),
            pl.BlockSpec((n_meta, d), lambda bi, i, j: (0, 0))]
        args = [x, x, meta]
    else:
        out_rows = seq
        last = p // n_meta - 1
        src_specs = [
            tile,
            pl.BlockSpec((None, n_meta, d),
                         lambda bi, i, j: (bi, jnp.minimum((i + 1) * per_tile, last), 0))]
        args = [src[0], src[0]]
    in_specs = src_specs + [vec,
                            pl.BlockSpec((d, tf), lambda bi, i, j: (0, j)),
                            pl.BlockSpec((d, tf), lambda bi, i, j: (0, j)),
                            pl.BlockSpec((tf, d), lambda bi, i, j: (j, 0))]
    args += [g, wg, wu, wd]
    if post:
        in_specs.append(vec)
        args.append(post_g)
    return pl.pallas_call(
        functools.partial(_ffn_kernel, source=source, post=post, nf=dff // tf,
                          n_meta=n_meta, length=length),
        out_shape=jax.ShapeDtypeStruct((b, out_rows, d), _F32),
        grid=(b, pl.cdiv(out_rows, tm), dff // tf),
        in_specs=in_specs,
        out_specs=tile,
        scratch_shapes=[pltpu.VMEM((tm, d), _BF16)],
        compiler_params=_params("parallel", "parallel", "arbitrary"),
        name="ffn_" + source + ("_post" if post else ""),
    )(*args)


def _inproj_kernel(h_ref, g_ref, w_ref, qg_ref, kg_ref, hsum_ref,
                   q_ref, k_ref, v_ref, u_ref, c_ref, *, aw, sw, cw):
    xn = _rms(h_ref[...], g_ref[...]).astype(_BF16)

    def proj(lo, width):
        return jnp.dot(xn, w_ref[:, lo:lo + width], preferred_element_type=_F32)

    def head_norm(t, gain):
        sq = (t * t).astype(_BF16)
        span = hsum_ref.shape[0]
        ss = jnp.concatenate(
            [jnp.dot(sq[:, lo:lo + span], hsum_ref[...], preferred_element_type=_F32)
             for lo in range(0, aw, span)], axis=1)
        return t * lax.rsqrt(ss * (1.0 / HEAD_DIM) + RMS_EPS) * gain

    q_ref[...] = head_norm(proj(0, aw), qg_ref[...]).astype(_BF16)
    k_ref[...] = head_norm(proj(aw, aw), kg_ref[...]).astype(_BF16)
    v_ref[...] = proj(2 * aw, aw).astype(_BF16)
    u_ref[...] = proj(3 * aw, sw)
    a = proj(3 * aw + sw, cw)
    gt = proj(3 * aw + sw + cw, cw)
    c_ref[...] = a * jax.nn.sigmoid(gt)


def _inproj(h, g, w, qg, kg, hsum, aw, sw, cw):
    n, d = h.shape
    tm = _pick_tile(n, 640, SUBLANES)
    row = lambda width: pl.BlockSpec((tm, width), lambda i: (i, 0))
    return pl.pallas_call(
        functools.partial(_inproj_kernel, aw=aw, sw=sw, cw=cw),
        out_shape=(jax.ShapeDtypeStruct((n, aw), _BF16),
                   jax.ShapeDtypeStruct((n, aw), _BF16),
                   jax.ShapeDtypeStruct((n, aw), _BF16),
                   jax.ShapeDtypeStruct((n, sw), _F32),
                   jax.ShapeDtypeStruct((n, cw), _F32)),
        grid=(n // tm,),
        in_specs=[row(d), _resident((1, d)), _resident(w.shape),
                  _resident((1, aw)), _resident((1, aw)), _resident(hsum.shape)],
        out_specs=(row(aw), row(aw), row(aw), row(sw), row(cw)),
        compiler_params=_params("parallel"),
        name="inproj",
    )(h, g, w, qg, kg, hsum)


def _attn_kernel(q_ref, k_ref, v_ref, tri_ref, o_ref, q2_ref, acc_ref, run_ref):
    tq = ATT_BLOCK
    npairs = q_ref.shape[1] // LANES
    qi = pl.program_id(1)
    first = lax.broadcasted_iota(jnp.int32, (tq, LANES), 1) < HEAD_DIM
    row_id = lax.broadcasted_iota(jnp.int32, (2 * tq, tq), 0)
    col_id = lax.broadcasted_iota(jnp.int32, (2 * tq, tq), 1)
    earlier = col_id < jnp.where(row_id >= tq, row_id - tq, row_id)

    for p in range(npairs):
        q = q_ref[:, p * LANES:(p + 1) * LANES]
        zero = jnp.zeros_like(q)
        q2_ref[p] = jnp.concatenate(
            [jnp.where(first, q, zero), jnp.where(first, zero, q)], axis=0)
    acc_ref[...] = jnp.zeros_like(acc_ref)
    run_ref[...] = jnp.zeros_like(run_ref)

    def sweep(blocks):
        pair_lanes = [slice(p * LANES, (p + 1) * LANES) for p in range(npairs)]
        keys = [pl.ds(pl.multiple_of(j * tq, tq), tq) for j, _ in blocks]
        tri = tri_ref[...]

        logits = [[lax.dot_general(q2_ref[p], k_ref[rows, pair_lanes[p]],
                                   (((1,), (1,)), ((), ())),
                                   preferred_element_type=_F32)
                   for p in range(npairs)] for rows in keys]

        partial = []
        for (_, masked), block_logits in zip(blocks, logits):
            per_pair = []
            for z in block_logits:
                nz = -z
                log_keep = jnp.minimum(nz, 0.0) - jnp.log2(1.0 + jnp.exp2(jnp.minimum(z, nz)))
                if masked:
                    log_keep = jnp.where(earlier, log_keep, 0.0)
                hi = log_keep.astype(_BF16)
                lo = (log_keep - hi.astype(_F32)).astype(_BF16)
                sums = jnp.dot(jnp.concatenate([hi, lo], axis=1), tri,
                               preferred_element_type=_F32)
                per_pair.append((log_keep + z, sums))
            partial.append(per_pair)

        mass = None
        for p in range(npairs):
            run = run_ref[p]
            out = None
            for (_, masked), rows, per_pair in zip(blocks, keys, partial):
                log_hit, sums = per_pair[p]
                att = jnp.exp2(log_hit + sums[:, :tq] + run)
                if masked:
                    att = jnp.where(earlier, att, 0.0)
                att2 = jnp.concatenate([att[:tq], att[tq:]], axis=1).astype(_BF16)
                vb = v_ref[rows, pair_lanes[p]]
                vzero = jnp.zeros_like(vb)
                v2 = jnp.concatenate(
                    [jnp.where(first, vb, vzero), jnp.where(first, vzero, vb)], axis=0)
                pv = jnp.dot(att2, v2, preferred_element_type=_F32)
                out = pv if out is None else out + pv
                run = run + sums[:, tq:]
            acc_ref[:, pair_lanes[p]] += out
            run_ref[p] = run
            mass = run if mass is None else jnp.maximum(mass, run)
        return jnp.max(mass)

    mass = lax.cond(qi > 0,
                    lambda: sweep([(qi, True), (qi - 1, False)]),
                    lambda: sweep([(qi, True)]))

    def more(state):
        j, live = state
        return jnp.logical_and(j >= 0, live)

    def step(state):
        j, _ = state
        return j - 1, sweep([(j, False)]) > ATT_LOG2_MASS_FLOOR

    lax.while_loop(more, step, (qi - 2, mass > ATT_LOG2_MASS_FLOOR))
    o_ref[...] = acc_ref[...].astype(o_ref.dtype)


def _attention(q, k, v, tri):
    b, p, aw = q.shape
    tq = ATT_BLOCK
    npairs = aw // LANES
    blk = pl.BlockSpec((None, tq, aw), lambda bi, i: (bi, i, 0))
    full = pl.BlockSpec((None, p, aw), lambda bi, i: (bi, 0, 0))
    return pl.pallas_call(
        _attn_kernel,
        out_shape=jax.ShapeDtypeStruct((b, p, aw), _BF16),
        grid=(b, p // tq),
        in_specs=[blk, full, full, _resident(tri.shape)],
        out_specs=blk,
        scratch_shapes=[pltpu.VMEM((npairs, 2 * tq, LANES), _BF16),
                        pltpu.VMEM((tq, aw), _F32),
                        pltpu.VMEM((npairs, 2 * tq, LANES), _F32)],
        compiler_params=_params("parallel", "arbitrary"),
        name="stickbreak_attn",
    )(q, k, v, tri)


def _ssm_kernel(u_ref, kmat_ref, bmat_ref, emat_ref, apow_ref, d_ref, y_ref,
                uc_ref, s_ref):
    t = SSM_CHUNK
    nc = uc_ref.shape[0]
    half = s_ref.shape[1] // 2

    for k in range(t):
        uc_ref[:, k * LANES:(k + 1) * LANES] = u_ref[pl.ds(k, nc, stride=t), :].astype(_BF16)
    s_ref[...] = jnp.dot(uc_ref[...], bmat_ref[...], preferred_element_type=_F32)

    pw_re, pw_im = apow_ref[0], apow_ref[1]
    sub = lax.broadcasted_iota(jnp.int32, (SUBLANES, half), 0)

    def cmul(ar, ai, xr, xi):
        return ar * xr - ai * xi, ar * xi + ai * xr

    def bcast_row(x, r):
        return jnp.broadcast_to(x[r:r + 1, :], x.shape)

    levels = []
    for k in (1, 2, 4):
        levels.append((k, jnp.where(sub >= k, bcast_row(pw_re, k - 1), 0.0),
                       jnp.where(sub >= k, bcast_row(pw_im, k - 1), 0.0)))

    def step(i, carry):
        in_re, in_im = carry
        r0 = pl.multiple_of(i * SUBLANES, SUBLANES)
        p_re = s_ref[pl.ds(r0, SUBLANES), 0:half]
        p_im = s_ref[pl.ds(r0, SUBLANES), half:2 * half]
        for k, ar, ai in levels:
            d_re, d_im = cmul(ar, ai, pltpu.roll(p_re, k, 0), pltpu.roll(p_im, k, 0))
            p_re, p_im = p_re + d_re, p_im + d_im
        c_re, c_im = cmul(pw_re, pw_im, in_re, in_im)
        out_re, out_im = p_re + c_re, p_im + c_im
        ent_re = jnp.where(sub >= 1, pltpu.roll(out_re, 1, 0), in_re)
        ent_im = jnp.where(sub >= 1, pltpu.roll(out_im, 1, 0), in_im)
        s_ref[pl.ds(r0, SUBLANES), 0:half] = ent_re
        s_ref[pl.ds(r0, SUBLANES), half:2 * half] = ent_im
        return bcast_row(out_re, SUBLANES - 1), bcast_row(out_im, SUBLANES - 1)

    zero = jnp.zeros((SUBLANES, half), _F32)
    lax.fori_loop(0, nc // SUBLANES, step, (zero, zero))

    y = jnp.dot(uc_ref[...], kmat_ref[...], preferred_element_type=_F32)
    y += jnp.dot(s_ref[...].astype(_BF16), emat_ref[...], preferred_element_type=_F32)
    for k in range(t):
        rows = pl.ds(k, nc, stride=t)
        y_ref[rows, :] = y[:, k * LANES:(k + 1) * LANES] + d_ref[...] * u_ref[rows, :]


def _ssm_matrices(lam_re, lam_im, log_step, b_re, b_im, c_re, c_im, d_skip):
    t = SSM_CHUNK
    g, p = lam_re.shape
    hg = b_re.shape[-1]
    gl = LANES // hg
    nb = g // gl
    hp = lax.Precision.HIGHEST
    lam = lax.complex(jnp.minimum(lam_re.astype(_F32), LAMBDA_RE_MAX), lam_im.astype(_F32))
    lam_dt = lam * jnp.exp(log_step.astype(_F32))[:, None]
    lam_bar = jnp.exp(lam_dt)
    b_bar = ((lam_bar - 1.0) / lam)[:, :, None] * lax.complex(b_re.astype(_F32), b_im.astype(_F32))
    c_cplx = lax.complex(c_re.astype(_F32), c_im.astype(_F32))
    steps = jnp.arange(t, dtype=_F32)
    powers = jnp.exp(lam_dt[None] * steps[:, None, None])
    eye = jnp.eye(gl, dtype=_F32)

    ktau = jnp.real(jnp.einsum('gop,tgp,gpi->gtio', c_cplx, powers, b_bar, precision=hp))
    kblk = jnp.einsum('jgtio,gk->jtgiko', ktau.reshape(nb, gl, t, hg, hg), eye)
    kblk = kblk.reshape(nb, t, LANES, LANES).astype(_BF16)
    lag = jnp.arange(t)[None, :] - jnp.arange(t)[:, None]
    kbig = jnp.where((lag >= 0)[None, :, :, None, None],
                     kblk[:, jnp.clip(lag, 0, t - 1)], 0.0)
    kmat = kbig.transpose(0, 1, 3, 2, 4).reshape(nb, t * LANES, t * LANES)

    bfull = (powers[::-1].transpose(1, 0, 2)[:, :, None, :]
             * b_bar.transpose(0, 2, 1)[:, None, :, :])
    bfull = bfull.reshape(nb, gl, t, hg, p)

    def spread_b(m):
        return jnp.einsum('jgthp,gk->jtghkp', m, eye).reshape(nb, t * LANES, gl * p)
    bmat = jnp.concatenate([spread_b(jnp.real(bfull)), spread_b(jnp.imag(bfull))], axis=2)

    efull = (c_cplx.transpose(0, 2, 1)[:, :, None, :]
             * (powers * lam_bar[None]).transpose(1, 2, 0)[:, :, :, None])
    efull = efull.reshape(nb, gl, p, t, hg)

    def spread_e(m):
        return jnp.einsum('jgpth,gk->jgptkh', m, eye).reshape(nb, gl * p, t * LANES)
    emat = jnp.concatenate([spread_e(jnp.real(efull)), spread_e(-jnp.imag(efull))], axis=1)

    rows = jnp.arange(1, SUBLANES + 1, dtype=_F32) * t
    apow = jnp.exp(lam_dt[None] * rows[:, None, None])
    apow = apow.reshape(SUBLANES, nb, gl * p).transpose(1, 0, 2)
    apow = jnp.stack([jnp.real(apow), jnp.imag(apow)], axis=1)
    return (kmat, bmat.astype(_BF16), emat.astype(_BF16), apow,
            d_skip.astype(_F32).reshape(nb, 1, LANES))


def _ssm(u, mats):
    b, p, sw = u.shape
    t = SSM_CHUNK
    nc = p // t
    assert nc % SUBLANES == 0 and sw % LANES == 0
    kmat, bmat, emat, apow, dvec = mats
    nb, lw, sl = bmat.shape
    col = pl.BlockSpec((None, p, LANES), lambda j, bi: (bi, 0, j))
    per = lambda *shape: pl.BlockSpec((None,) + shape, lambda j, bi: (j,) + (0,) * len(shape))
    return pl.pallas_call(
        _ssm_kernel,
        out_shape=jax.ShapeDtypeStruct(u.shape, _F32),
        grid=(nb, b),
        in_specs=[col, per(lw, lw), per(lw, sl), per(sl, lw),
                  per(2, SUBLANES, sl // 2), per(1, LANES)],
        out_specs=col,
        scratch_shapes=[pltpu.VMEM((nc, lw), _BF16), pltpu.VMEM((nc, sl), _F32)],
        compiler_params=_params("parallel", "parallel"),
        name="s5_ssm",
    )(u, kmat, bmat, emat, apow, dvec)


def _conv_kernel(halo_ref, x_ref, w_ref, b_ref, lg_ref, lb_ref, o_ref,
                 sh_ref, acc_ref, wb_ref, *, taps):
    tt, c = x_ref.shape
    halo = DW_TAPS_HALO
    lead = halo - (taps - 1)
    first_tile = pl.program_id(1) == 0
    sh_ref[0, 0:halo, :] = jnp.where(first_tile, 0.0, halo_ref[...])
    sh_ref[0, halo:halo + tt, :] = x_ref[...]
    body_rows = tt + halo - SUBLANES
    for s in range(1, SUBLANES):
        sh_ref[s, 0:body_rows, :] = sh_ref[0, s:s + body_rows, :]

    for k in range(taps):
        wb_ref[k] = jnp.broadcast_to(w_ref[k:k + 1, :], (SUBLANES, c))
    bias = jnp.broadcast_to(b_ref[...], (SUBLANES, c))
    halves = 4
    rows = halves * SUBLANES

    def tile(r, _):
        r0 = pl.multiple_of(r * rows, rows)
        acc = [bias] * halves
        for k in range(taps):
            a, s = divmod(lead + k, SUBLANES)
            w = wb_ref[k]
            for i in range(halves):
                acc[i] = acc[i] + w * sh_ref[s, pl.ds(r0 + (a + i) * SUBLANES, SUBLANES), :]
        for i in range(halves):
            acc_ref[pl.ds(r0 + i * SUBLANES, SUBLANES), :] = acc[i]
        return 0

    lax.fori_loop(0, tt // rows, tile, 0)

    y = acc_ref[...]
    yc = y - jnp.mean(y, axis=-1, keepdims=True)
    yn = yc * lax.rsqrt(jnp.mean(yc * yc, axis=-1, keepdims=True) + LN_EPS)
    yn = yn * lg_ref[...] + lb_ref[...]
    o_ref[...] = (yn * jax.nn.sigmoid(yn)).astype(o_ref.dtype)


def _conv(x, w, bias, ln_g, ln_b):
    b, p, c = x.shape
    taps = w.shape[0]
    halo = DW_TAPS_HALO
    assert taps - 1 <= halo
    tt = _pick_tile(p, 640, halo)
    per_tile = tt // halo
    vec = lambda r: pl.BlockSpec((r, c), lambda bi, i: (0, 0))
    return pl.pallas_call(
        functools.partial(_conv_kernel, taps=taps),
        out_shape=jax.ShapeDtypeStruct((b, p, c), _BF16),
        grid=(b, p // tt),
        in_specs=[pl.BlockSpec((None, halo, c),
                               lambda bi, i: (bi, jnp.maximum(i * per_tile - 1, 0), 0)),
                  pl.BlockSpec((None, tt, c), lambda bi, i: (bi, i, 0)),
                  vec(taps), vec(1), vec(1), vec(1)],
        out_specs=pl.BlockSpec((None, tt, c), lambda bi, i: (bi, i, 0)),
        scratch_shapes=[pltpu.VMEM((SUBLANES, tt + halo, c), _F32),
                        pltpu.VMEM((tt, c), _F32),
                        pltpu.VMEM((taps, SUBLANES, c), _F32)],
        compiler_params=_params("parallel", "arbitrary"),
        name="dwconv_ln_swish",
    )(x, x, w, bias, ln_g, ln_b)


def _gelu_tanh(x):
    c = 0.7978845608028654
    return 0.5 * x * (1.0 + jnp.tanh(c * (x + 0.044715 * (x * x * x))))


def _outproj_kernel(h_ref, a_ref, y_ref, c_ref, wglu_ref, bglu_ref, wpw_ref,
                    bpw_ref, wo_ref, o_ref, *, aw, sw):
    y = _gelu_tanh(y_ref[...])
    gate = jnp.dot(y.astype(_BF16), wglu_ref[...], preferred_element_type=_F32) + bglu_ref[...]
    ssm = y * jax.nn.sigmoid(gate)
    conv = jnp.dot(c_ref[...], wpw_ref[...], preferred_element_type=_F32) + bpw_ref[...]
    mixed = jnp.dot(a_ref[...], wo_ref[0:aw, :], preferred_element_type=_F32)
    mixed += jnp.dot(ssm.astype(_BF16), wo_ref[aw:aw + sw, :], preferred_element_type=_F32)
    mixed += jnp.dot(conv.astype(_BF16), wo_ref[aw + sw:, :], preferred_element_type=_F32)
    o_ref[...] = h_ref[...] + mixed


def _outproj(h, attn, y, conv, wglu, bglu, wpw, bpw, wo):
    n, d = h.shape
    aw, sw, cw = attn.shape[1], y.shape[1], conv.shape[1]
    tm = _pick_tile(n, 640, SUBLANES)
    row = lambda width: pl.BlockSpec((tm, width), lambda i: (i, 0))
    return pl.pallas_call(
        functools.partial(_outproj_kernel, aw=aw, sw=sw),
        out_shape=jax.ShapeDtypeStruct((n, d), _F32),
        grid=(n // tm,),
        in_specs=[row(d), row(aw), row(sw), row(cw),
                  _resident(wglu.shape), _resident((1, sw)),
                  _resident(wpw.shape), _resident((1, cw)), _resident(wo.shape)],
        out_specs=row(d),
        compiler_params=_params("parallel"),
        name="outproj",
    )(h, attn, y, conv, wglu, bglu, wpw, bpw, wo)


def kernel(x, meta, ffn1_norm, ffn1_w_gate, ffn1_w_up, ffn1_w_down, mix_norm, w_in, q_norm, k_norm, ssm_lambda_re, ssm_lambda_im, ssm_log_step, ssm_b_re, ssm_b_im, ssm_c_re, ssm_c_im, ssm_d, ssm_w_glu, ssm_b_glu, conv_w_dw, conv_b_dw, conv_ln_g, conv_ln_b, conv_w_pw, conv_b_pw, w_out, ffn2_norm, ffn2_w_gate, ffn2_w_up, ffn2_w_down, post_norm):
    batch, seq, d = x.shape
    depth = ffn1_norm.shape[0]
    n_meta = meta.shape[0]
    length = seq + n_meta
    sw = ssm_w_glu.shape[1]
    cw = conv_w_pw.shape[1]
    aw = (w_in.shape[2] - sw - 2 * cw) // 3
    heads = aw // HEAD_DIM
    assert aw % LANES == 0 and sw % LANES == 0 and LANES % SSM_GROUP == 0

    p = -(-length // ATT_BLOCK) * ATT_BLOCK
    n = batch * p

    span = MXU_WIDTH if aw % MXU_WIDTH == 0 else LANES
    head_sum = jnp.kron(jnp.eye(span // HEAD_DIM, dtype=_F32),
                        jnp.ones((HEAD_DIM, HEAD_DIM), _F32)).astype(_BF16)
    ids = jnp.arange(ATT_BLOCK)
    tri = jnp.concatenate([(ids[:, None] > ids[None, :]).astype(_BF16),
                           jnp.ones((ATT_BLOCK, ATT_BLOCK), _BF16)], axis=1)
    tri = jnp.concatenate([tri, tri], axis=0)
    logit_scale = HEAD_DIM ** -0.5 * math.log2(math.e)
    vec = lambda a: a.astype(_F32).reshape(1, -1)

    for l in range(depth):
        first, last = l == 0, l == depth - 1
        h = _ffn((x, meta.astype(x.dtype)) if first else (h,), vec(ffn1_norm[l]),
                 _layer_bf16(ffn1_w_gate, l), _layer_bf16(ffn1_w_up, l),
                 _layer_bf16(ffn1_w_down, l),
                 source="embed" if first else "rows", p=p, n_meta=n_meta)
        h = h.reshape(n, d)

        q, k, v, u, ch = _inproj(
            h, vec(mix_norm[l]), _layer_bf16(w_in, l),
            vec(jnp.tile(q_norm[l].astype(_F32) * logit_scale, heads)),
            vec(jnp.tile(k_norm[l].astype(_F32), heads)), head_sum, aw, sw, cw)

        attn = _attention(q.reshape(batch, p, aw), k.reshape(batch, p, aw),
                          v.reshape(batch, p, aw), tri)
        mats = _ssm_matrices(ssm_lambda_re[l], ssm_lambda_im[l], ssm_log_step[l],
                             ssm_b_re[l], ssm_b_im[l], ssm_c_re[l], ssm_c_im[l], ssm_d[l])
        y = _ssm(u.reshape(batch, p, sw), mats)
        conv = _conv(ch.reshape(batch, p, cw), conv_w_dw[l].astype(_F32),
                     vec(conv_b_dw[l]), vec(conv_ln_g[l]), vec(conv_ln_b[l]))

        h = _outproj(h, attn.reshape(n, aw), y.reshape(n, sw), conv.reshape(n, cw),
                     _layer_bf16(ssm_w_glu, l), vec(ssm_b_glu[l]),
                     _layer_bf16(conv_w_pw, l), vec(conv_b_pw[l]),
                     _layer_bf16(w_out, l))

        h = _ffn((h.reshape(batch, p, d),), vec(ffn2_norm[l]),
                 _layer_bf16(ffn2_w_gate, l), _layer_bf16(ffn2_w_up, l),
                 _layer_bf16(ffn2_w_down, l), post_g=vec(post_norm[l]),
                 source="shifted" if last else "rows", p=p, seq=seq, n_meta=n_meta)

    return h
```

```python
import functools
import math

import jax
import jax.numpy as jnp
from jax import lax
from jax.experimental import pallas as pl
from jax.experimental.pallas import tpu as pltpu

_F32 = jnp.float32
_BF16 = jnp.bfloat16

HEAD_DIM = 64
SSM_GROUP = 16
DW_TAPS_HALO = 32
RMS_EPS = 1e-6
LN_EPS = 1e-5
LAMBDA_RE_MAX = -1e-4
LANES = 128
SUBLANES = 8
MXU_WIDTH = 256
SSM_CHUNK = SUBLANES
ATT_BLOCK = 128
ATT_LOG2_MASS_FLOOR = -40.0 * math.log2(math.e)
VMEM_LIMIT_BYTES = 56 * 1024 * 1024
CAST_BLOCK_BYTES = 6 * 1024 * 1024


def _pick_tile(n, target, mult):
    best = None
    for t in range(mult, min(n, target) + 1, mult):
        if n % t == 0:
            best = t
    assert best is not None, (n, target, mult)
    return best


def _params(*sem):
    return pltpu.CompilerParams(dimension_semantics=sem,
                                vmem_limit_bytes=VMEM_LIMIT_BYTES)


def _rms(x, g):
    ms = jnp.mean(x * x, axis=-1, keepdims=True)
    return x * lax.rsqrt(ms + RMS_EPS) * g


def _resident(shape):
    zeros = (0,) * len(shape)
    return pl.BlockSpec(shape, lambda *_: zeros, pipeline_mode=pl.Buffered(1))


def _cast_kernel(w_ref, o_ref):
    o_ref[...] = w_ref[...].astype(o_ref.dtype)


def _layer_bf16(w, l):
    _, r, c = w.shape
    pack = 2 * SUBLANES
    tr = _pick_tile(r, max(pack, CAST_BLOCK_BYTES // (c * w.dtype.itemsize)), pack)
    return pl.pallas_call(
        _cast_kernel,
        out_shape=jax.ShapeDtypeStruct((r, c), _BF16),
        grid=(r // tr,),
        in_specs=[pl.BlockSpec((None, tr, c), lambda i: (l, i, 0))],
        out_specs=pl.BlockSpec((tr, c), lambda i: (i, 0)),
        compiler_params=_params("parallel"),
        name="cast_bf16",
    )(w)


_FFN_SOURCE_REFS = {"rows": 1, "embed": 3, "shifted": 2}


def _ffn_kernel(*refs, source, post, nf, n_meta, length):
    n_src = _FFN_SOURCE_REFS[source]
    src = refs[:n_src]
    g_ref, wg_ref, wu_ref, wd_ref = refs[n_src:n_src + 4]
    pg_ref = refs[n_src + 4] if post else None
    o_ref, xn_ref = refs[-2:]
    i = pl.program_id(1)
    f = pl.program_id(2)
    tm = o_ref.shape[0]

    @pl.when(f == 0)
    def _():
        if source == "rows":
            x = src[0][...]
        elif source == "embed":
            tail_ref, body_ref, meta_ref = src
            head = jnp.where(i == 0, meta_ref[...], tail_ref[...])
            x = jnp.concatenate([head, body_ref[0:tm - n_meta, :]], axis=0)
            row = i * tm + lax.broadcasted_iota(jnp.int32, (tm, 1), 0)
            x = jnp.where(row < length, x, 0.0)
        else:
            body_ref, next_ref = src
            x = jnp.concatenate([body_ref[n_meta:, :], next_ref[...]], axis=0)
        xn_ref[...] = _rms(x, g_ref[...]).astype(_BF16)
        o_ref[...] = x

    xn = xn_ref[...]
    gate = jnp.dot(xn, wg_ref[...], preferred_element_type=_F32)
    up = jnp.dot(xn, wu_ref[...], preferred_element_type=_F32)
    act = (gate * jax.nn.sigmoid(gate)) * (up * 0.5)
    o_ref[...] += jnp.dot(act.astype(_BF16), wd_ref[...],
                          preferred_element_type=_F32)

    if post:
        @pl.when(f == nf - 1)
        def _():
            o_ref[...] = _rms(o_ref[...], pg_ref[...])


def _ffn(src, g, wg, wu, wd, post_g=None, source="rows", p=None, seq=None, n_meta=SUBLANES):
    b, rows_in, d = src[0].shape
    p = rows_in if p is None else p
    dff = wg.shape[1]
    assert n_meta % SUBLANES == 0
    tm = _pick_tile(p, 1040 if source == "rows" else 832, n_meta)
    tf = _pick_tile(dff, 512, LANES)
    per_tile = tm // n_meta
    post = post_g is not None
    tile = pl.BlockSpec((None, tm, d), lambda bi, i, j: (bi, i, 0))
    vec = pl.BlockSpec((1, d), lambda bi, i, j: (0, 0))
    length = 0
    if source == "rows":
        out_rows = p
        src_specs, args = [tile], [src[0]]
    elif source == "embed":
        x, meta = src
        length = rows_in + n_meta
        out_rows = p
        last = pl.cdiv(rows_in, tm) - 1
        src_specs = [
            pl.BlockSpec((None, n_meta, d),
                         lambda bi, i, j: (bi, jnp.maximum(i * per_tile - 1, 0), 0)),
            pl.BlockSpec((None, tm, d), lambda bi, i, j: (bi, jnp.minimum(i, last), 0)),
            pl.BlockSpec((n_meta, d), lambda bi, i, j: (0, 0))]
        args = [x, x, meta]
    else:
        out_rows = seq
        last = p // n_meta - 1
        src_specs = [
            tile,
            pl.BlockSpec((None, n_meta, d),
                         lambda bi, i, j: (bi, jnp.minimum((i + 1) * per_tile, last), 0))]
        args = [src[0], src[0]]
    in_specs = src_specs + [vec,
                            pl.BlockSpec((d, tf), lambda bi, i, j: (0, j)),
                            pl.BlockSpec((d, tf), lambda bi, i, j: (0, j)),
                            pl.BlockSpec((tf, d), lambda bi, i, j: (j, 0))]
    args += [g, wg, wu, wd]
    if post:
        in_specs.append(vec)
        args.append(post_g)
    return pl.pallas_call(
        functools.partial(_ffn_kernel, source=source, post=post, nf=dff // tf,
                          n_meta=n_meta, length=length),
        out_shape=jax.ShapeDtypeStruct((b, out_rows, d), _F32),
        grid=(b, pl.cdiv(out_rows, tm), dff // tf),
        in_specs=in_specs,
        out_specs=tile,
        scratch_shapes=[pltpu.VMEM((tm, d), _BF16)],
        compiler_params=_params("parallel", "parallel", "arbitrary"),
        name="ffn_" + source + ("_post" if post else ""),
    )(*args)


def _inproj_kernel(h_ref, g_ref, w_ref, qg_ref, kg_ref, hsum_ref,
                   q_ref, k_ref, v_ref, u_ref, c_ref, *, aw, sw, cw):
    xn = _rms(h_ref[...], g_ref[...]).astype(_BF16)

    def proj(lo, width):
        return jnp.dot(xn, w_ref[:, lo:lo + width], preferred_element_type=_F32)

    def head_norm(t, gain):
        sq = (t * t).astype(_BF16)
        span = hsum_ref.shape[0]
        ss = jnp.concatenate(
            [jnp.dot(sq[:, lo:lo + span], hsum_ref[...], preferred_element_type=_F32)
             for lo in range(0, aw, span)], axis=1)
        return t * lax.rsqrt(ss * (1.0 / HEAD_DIM) + RMS_EPS) * gain

    q_ref[...] = head_norm(proj(0, aw), qg_ref[...]).astype(_BF16)
    k_ref[...] = head_norm(proj(aw, aw), kg_ref[...]).astype(_BF16)
    v_ref[...] = proj(2 * aw, aw).astype(_BF16)
    u_ref[...] = proj(3 * aw, sw)
    a = proj(3 * aw + sw, cw)
    gt = proj(3 * aw + sw + cw, cw)
    c_ref[...] = a * jax.nn.sigmoid(gt)


def _inproj(h, g, w, qg, kg, hsum, aw, sw, cw):
    n, d = h.shape
    tm = _pick_tile(n, 640, SUBLANES)
    row = lambda width: pl.BlockSpec((tm, width), lambda i: (i, 0))
    return pl.pallas_call(
        functools.partial(_inproj_kernel, aw=aw, sw=sw, cw=cw),
        out_shape=(jax.ShapeDtypeStruct((n, aw), _BF16),
                   jax.ShapeDtypeStruct((n, aw), _BF16),
                   jax.ShapeDtypeStruct((n, aw), _BF16),
                   jax.ShapeDtypeStruct((n, sw), _F32),
                   jax.ShapeDtypeStruct((n, cw), _F32)),
        grid=(n // tm,),
        in_specs=[row(d), _resident((1, d)), _resident(w.shape),
                  _resident((1, aw)), _resident((1, aw)), _resident(hsum.shape)],
        out_specs=(row(aw), row(aw), row(aw), row(sw), row(cw)),
        compiler_params=_params("parallel"),
        name="inproj",
    )(h, g, w, qg, kg, hsum)


def _attn_kernel(q_ref, k_ref, v_ref, tri_ref, o_ref, q2_ref, acc_ref, run_ref):
    tq = ATT_BLOCK
    npairs = q_ref.shape[1] // LANES
    qi = pl.program_id(1)
    first = lax.broadcasted_iota(jnp.int32, (tq, LANES), 1) < HEAD_DIM
    row_id = lax.broadcasted_iota(jnp.int32, (2 * tq, tq), 0)
    col_id = lax.broadcasted_iota(jnp.int32, (2 * tq, tq), 1)
    earlier = col_id < jnp.where(row_id >= tq, row_id - tq, row_id)

    for p in range(npairs):
        q = q_ref[:, p * LANES:(p + 1) * LANES]
        zero = jnp.zeros_like(q)
        q2_ref[p] = jnp.concatenate(
            [jnp.where(first, q, zero), jnp.where(first, zero, q)], axis=0)
    acc_ref[...] = jnp.zeros_like(acc_ref)
    run_ref[...] = jnp.zeros_like(run_ref)

    def sweep(blocks):
        pair_lanes = [slice(p * LANES, (p + 1) * LANES) for p in range(npairs)]
        keys = [pl.ds(pl.multiple_of(j * tq, tq), tq) for j, _ in blocks]
        tri = tri_ref[...]

        logits = [[lax.dot_general(q2_ref[p], k_ref[rows, pair_lanes[p]],
                                   (((1,), (1,)), ((), ())),
                                   preferred_element_type=_F32)
                   for p in range(npairs)] for rows in keys]

        partial = []
        for (_, masked), block_logits in zip(blocks, logits):
            per_pair = []
            for z in block_logits:
                nz = -z
                log_keep = jnp.minimum(nz, 0.0) - jnp.log2(1.0 + jnp.exp2(jnp.minimum(z, nz)))
                if masked:
                    log_keep = jnp.where(earlier, log_keep, 0.0)
                hi = log_keep.astype(_BF16)
                lo = (log_keep - hi.astype(_F32)).astype(_BF16)
                sums = jnp.dot(jnp.concatenate([hi, lo], axis=1), tri,
                               preferred_element_type=_F32)
                per_pair.append((log_keep + z, sums))
            partial.append(per_pair)

        mass = None
        for p in range(npairs):
            run = run_ref[p]
            out = None
            for (_, masked), rows, per_pair in zip(blocks, keys, partial):
                log_hit, sums = per_pair[p]
                att = jnp.exp2(log_hit + sums[:, :tq] + run)
                if masked:
                    att = jnp.where(earlier, att, 0.0)
                att2 = jnp.concatenate([att[:tq], att[tq:]], axis=1).astype(_BF16)
                vb = v_ref[rows, pair_lanes[p]]
                vzero = jnp.zeros_like(vb)
                v2 = jnp.concatenate(
                    [jnp.where(first, vb, vzero), jnp.where(first, vzero, vb)], axis=0)
                pv = jnp.dot(att2, v2, preferred_element_type=_F32)
                out = pv if out is None else out + pv
                run = run + sums[:, tq:]
            acc_ref[:, pair_lanes[p]] += out
            run_ref[p] = run
            mass = run if mass is None else jnp.maximum(mass, run)
        return jnp.max(mass)

    mass = lax.cond(qi > 0,
                    lambda: sweep([(qi, True), (qi - 1, False)]),
                    lambda: sweep([(qi, True)]))

    def more(state):
        j, live = state
        return jnp.logical_and(j >= 0, live)

    def step(state):
        j, _ = state
        return j - 1, sweep([(j, False)]) > ATT_LOG2_MASS_FLOOR

    lax.while_loop(more, step, (qi - 2, mass > ATT_LOG2_MASS_FLOOR))
    o_ref[...] = acc_ref[...].astype(o_ref.dtype)


def _attention(q, k, v, tri):
    b, p, aw = q.shape
    tq = ATT_BLOCK
    npairs = aw // LANES
    blk = pl.BlockSpec((None, tq, aw), lambda bi, i: (bi, i, 0))
    full = pl.BlockSpec((None, p, aw), lambda bi, i: (bi, 0, 0))
    return pl.pallas_call(
        _attn_kernel,
        out_shape=jax.ShapeDtypeStruct((b, p, aw), _BF16),
        grid=(b, p // tq),
        in_specs=[blk, full, full, _resident(tri.shape)],
        out_specs=blk,
        scratch_shapes=[pltpu.VMEM((npairs, 2 * tq, LANES), _BF16),
                        pltpu.VMEM((tq, aw), _F32),
                        pltpu.VMEM((npairs, 2 * tq, LANES), _F32)],
        compiler_params=_params("parallel", "arbitrary"),
        name="stickbreak_attn",
    )(q, k, v, tri)


def _ssm_kernel(u_ref, kmat_ref, bmat_ref, emat_ref, apow_ref, d_ref, y_ref,
                uc_ref, s_ref):
    t = SSM_CHUNK
    nc = uc_ref.shape[0]
    half = s_ref.shape[1] // 2

    for k in range(t):
        uc_ref[:, k * LANES:(k + 1) * LANES] = u_ref[pl.ds(k, nc, stride=t), :].astype(_BF16)
    s_ref[...] = jnp.dot(uc_ref[...], bmat_ref[...], preferred_element_type=_F32)

    pw_re, pw_im = apow_ref[0], apow_ref[1]
    sub = lax.broadcasted_iota(jnp.int32, (SUBLANES, half), 0)

    def cmul(ar, ai, xr, xi):
        return ar * xr - ai * xi, ar * xi + ai * xr

    def bcast_row(x, r):
        return jnp.broadcast_to(x[r:r + 1, :], x.shape)

    levels = []
    for k in (1, 2, 4):
        levels.append((k, jnp.where(sub >= k, bcast_row(pw_re, k - 1), 0.0),
                       jnp.where(sub >= k, bcast_row(pw_im, k - 1), 0.0)))

    def step(i, carry):
        in_re, in_im = carry
        r0 = pl.multiple_of(i * SUBLANES, SUBLANES)
        p_re = s_ref[pl.ds(r0, SUBLANES), 0:half]
        p_im = s_ref[pl.ds(r0, SUBLANES), half:2 * half]
        for k, ar, ai in levels:
            d_re, d_im = cmul(ar, ai, pltpu.roll(p_re, k, 0), pltpu.roll(p_im, k, 0))
            p_re, p_im = p_re + d_re, p_im + d_im
        c_re, c_im = cmul(pw_re, pw_im, in_re, in_im)
        out_re, out_im = p_re + c_re, p_im + c_im
        ent_re = jnp.where(sub >= 1, pltpu.roll(out_re, 1, 0), in_re)
        ent_im = jnp.where(sub >= 1, pltpu.roll(out_im, 1, 0), in_im)
        s_ref[pl.ds(r0, SUBLANES), 0:half] = ent_re
        s_ref[pl.ds(r0, SUBLANES), half:2 * half] = ent_im
        return bcast_row(out_re, SUBLANES - 1), bcast_row(out_im, SUBLANES - 1)

    zero = jnp.zeros((SUBLANES, half), _F32)
    lax.fori_loop(0, nc // SUBLANES, step, (zero, zero))

    width = uc_ref.shape[1]
    y = jnp.concatenate(
        [jnp.dot(uc_ref[:, :lo + MXU_WIDTH], kmat_ref[:lo + MXU_WIDTH, lo:lo + MXU_WIDTH],
                 preferred_element_type=_F32) for lo in range(0, width, MXU_WIDTH)], axis=1)
    y += jnp.dot(s_ref[...].astype(_BF16), emat_ref[...], preferred_element_type=_F32)
    for k in range(t):
        rows = pl.ds(k, nc, stride=t)
        y_ref[rows, :] = y[:, k * LANES:(k + 1) * LANES] + d_ref[...] * u_ref[rows, :]


def _ssm_matrices(lam_re, lam_im, log_step, b_re, b_im, c_re, c_im, d_skip):
    t = SSM_CHUNK
    g, p = lam_re.shape
    hg = b_re.shape[-1]
    gl = LANES // hg
    nb = g // gl
    hp = lax.Precision.HIGHEST
    lam = lax.complex(jnp.minimum(lam_re.astype(_F32), LAMBDA_RE_MAX), lam_im.astype(_F32))
    lam_dt = lam * jnp.exp(log_step.astype(_F32))[:, None]
    lam_bar = jnp.exp(lam_dt)
    b_bar = ((lam_bar - 1.0) / lam)[:, :, None] * lax.complex(b_re.astype(_F32), b_im.astype(_F32))
    c_cplx = lax.complex(c_re.astype(_F32), c_im.astype(_F32))
    steps = jnp.arange(t, dtype=_F32)
    powers = jnp.exp(lam_dt[None] * steps[:, None, None])
    eye = jnp.eye(gl, dtype=_F32)

    ktau = jnp.real(jnp.einsum('gop,tgp,gpi->gtio', c_cplx, powers, b_bar, precision=hp))
    kblk = jnp.einsum('jgtio,gk->jtgiko', ktau.reshape(nb, gl, t, hg, hg), eye)
    kblk = kblk.reshape(nb, t, LANES, LANES).astype(_BF16)
    lag = jnp.arange(t)[None, :] - jnp.arange(t)[:, None]
    kbig = jnp.where((lag >= 0)[None, :, :, None, None],
                     kblk[:, jnp.clip(lag, 0, t - 1)], 0.0)
    kmat = kbig.transpose(0, 1, 3, 2, 4).reshape(nb, t * LANES, t * LANES)

    bfull = (powers[::-1].transpose(1, 0, 2)[:, :, None, :]
             * b_bar.transpose(0, 2, 1)[:, None, :, :])
    bfull = bfull.reshape(nb, gl, t, hg, p)

    def spread_b(m):
        return jnp.einsum('jgthp,gk->jtghkp', m, eye).reshape(nb, t * LANES, gl * p)
    bmat = jnp.concatenate([spread_b(jnp.real(bfull)), spread_b(jnp.imag(bfull))], axis=2)

    efull = (c_cplx.transpose(0, 2, 1)[:, :, None, :]
             * (powers * lam_bar[None]).transpose(1, 2, 0)[:, :, :, None])
    efull = efull.reshape(nb, gl, p, t, hg)

    def spread_e(m):
        return jnp.einsum('jgpth,gk->jgptkh', m, eye).reshape(nb, gl * p, t * LANES)
    emat = jnp.concatenate([spread_e(jnp.real(efull)), spread_e(-jnp.imag(efull))], axis=1)

    rows = jnp.arange(1, SUBLANES + 1, dtype=_F32) * t
    apow = jnp.exp(lam_dt[None] * rows[:, None, None])
    apow = apow.reshape(SUBLANES, nb, gl * p).transpose(1, 0, 2)
    apow = jnp.stack([jnp.real(apow), jnp.imag(apow)], axis=1)
    return (kmat, bmat.astype(_BF16), emat.astype(_BF16), apow,
            d_skip.astype(_F32).reshape(nb, 1, LANES))


def _ssm(u, mats):
    b, p, sw = u.shape
    t = SSM_CHUNK
    nc = p // t
    assert nc % SUBLANES == 0 and sw % LANES == 0
    kmat, bmat, emat, apow, dvec = mats
    nb, lw, sl = bmat.shape
    col = pl.BlockSpec((None, p, LANES), lambda j, bi: (bi, 0, j))
    per = lambda *shape: pl.BlockSpec((None,) + shape, lambda j, bi: (j,) + (0,) * len(shape))
    return pl.pallas_call(
        _ssm_kernel,
        out_shape=jax.ShapeDtypeStruct(u.shape, _F32),
        grid=(nb, b),
        in_specs=[col, per(lw, lw), per(lw, sl), per(sl, lw),
                  per(2, SUBLANES, sl // 2), per(1, LANES)],
        out_specs=col,
        scratch_shapes=[pltpu.VMEM((nc, lw), _BF16), pltpu.VMEM((nc, sl), _F32)],
        compiler_params=_params("parallel", "parallel"),
        name="s5_ssm",
    )(u, kmat, bmat, emat, apow, dvec)


def _conv_kernel(halo_ref, x_ref, w_ref, b_ref, lg_ref, lb_ref, o_ref,
                 sh_ref, acc_ref, wb_ref, *, taps):
    tt, c = x_ref.shape
    halo = DW_TAPS_HALO
    lead = halo - (taps - 1)
    first_tile = pl.program_id(1) == 0
    sh_ref[0, 0:halo, :] = jnp.where(first_tile, 0.0, halo_ref[...])
    sh_ref[0, halo:halo + tt, :] = x_ref[...]
    body_rows = tt + halo - SUBLANES
    for s in range(1, SUBLANES):
        sh_ref[s, 0:body_rows, :] = sh_ref[0, s:s + body_rows, :]

    for k in range(taps):
        wb_ref[k] = jnp.broadcast_to(w_ref[k:k + 1, :], (SUBLANES, c))
    bias = jnp.broadcast_to(b_ref[...], (SUBLANES, c))
    halves = 4
    rows = halves * SUBLANES

    def tile(r, _):
        r0 = pl.multiple_of(r * rows, rows)
        acc = [bias] * halves
        for k in range(taps):
            a, s = divmod(lead + k, SUBLANES)
            w = wb_ref[k]
            for i in range(halves):
                acc[i] = acc[i] + w * sh_ref[s, pl.ds(r0 + (a + i) * SUBLANES, SUBLANES), :]
        for i in range(halves):
            acc_ref[pl.ds(r0 + i * SUBLANES, SUBLANES), :] = acc[i]
        return 0

    lax.fori_loop(0, tt // rows, tile, 0)

    y = acc_ref[...]
    yc = y - jnp.mean(y, axis=-1, keepdims=True)
    yn = yc * lax.rsqrt(jnp.mean(yc * yc, axis=-1, keepdims=True) + LN_EPS)
    yn = yn * lg_ref[...] + lb_ref[...]
    o_ref[...] = (yn * jax.nn.sigmoid(yn)).astype(o_ref.dtype)


def _conv(x, w, bias, ln_g, ln_b):
    b, p, c = x.shape
    taps = w.shape[0]
    halo = DW_TAPS_HALO
    assert taps - 1 <= halo
    tt = _pick_tile(p, 640, halo)
    per_tile = tt // halo
    vec = lambda r: pl.BlockSpec((r, c), lambda bi, i: (0, 0))
    return pl.pallas_call(
        functools.partial(_conv_kernel, taps=taps),
        out_shape=jax.ShapeDtypeStruct((b, p, c), _BF16),
        grid=(b, p // tt),
        in_specs=[pl.BlockSpec((None, halo, c),
                               lambda bi, i: (bi, jnp.maximum(i * per_tile - 1, 0), 0)),
                  pl.BlockSpec((None, tt, c), lambda bi, i: (bi, i, 0)),
                  vec(taps), vec(1), vec(1), vec(1)],
        out_specs=pl.BlockSpec((None, tt, c), lambda bi, i: (bi, i, 0)),
        scratch_shapes=[pltpu.VMEM((SUBLANES, tt + halo, c), _F32),
                        pltpu.VMEM((tt, c), _F32),
                        pltpu.VMEM((taps, SUBLANES, c), _F32)],
        compiler_params=_params("parallel", "arbitrary"),
        name="dwconv_ln_swish",
    )(x, x, w, bias, ln_g, ln_b)


def _gelu_tanh(x):
    c = 0.7978845608028654
    return 0.5 * x * (1.0 + jnp.tanh(c * (x + 0.044715 * (x * x * x))))


def _outproj_kernel(h_ref, a_ref, y_ref, c_ref, wglu_ref, bglu_ref, wpw_ref,
                    bpw_ref, wo_ref, o_ref, *, aw, sw):
    y = _gelu_tanh(y_ref[...])
    gate = jnp.dot(y.astype(_BF16), wglu_ref[...], preferred_element_type=_F32) + bglu_ref[...]
    ssm = y * jax.nn.sigmoid(gate)
    conv = jnp.dot(c_ref[...], wpw_ref[...], preferred_element_type=_F32) + bpw_ref[...]
    mixed = jnp.dot(a_ref[...], wo_ref[0:aw, :], preferred_element_type=_F32)
    mixed += jnp.dot(ssm.astype(_BF16), wo_ref[aw:aw + sw, :], preferred_element_type=_F32)
    mixed += jnp.dot(conv.astype(_BF16), wo_ref[aw + sw:, :], preferred_element_type=_F32)
    o_ref[...] = h_ref[...] + mixed


def _outproj(h, attn, y, conv, wglu, bglu, wpw, bpw, wo):
    n, d = h.shape
    aw, sw, cw = attn.shape[1], y.shape[1], conv.shape[1]
    tm = _pick_tile(n, 640, SUBLANES)
    row = lambda width: pl.BlockSpec((tm, width), lambda i: (i, 0))
    return pl.pallas_call(
        functools.partial(_outproj_kernel, aw=aw, sw=sw),
        out_shape=jax.ShapeDtypeStruct((n, d), _F32),
        grid=(n // tm,),
        in_specs=[row(d), row(aw), row(sw), row(cw),
                  _resident(wglu.shape), _resident((1, sw)),
                  _resident(wpw.shape), _resident((1, cw)), _resident(wo.shape)],
        out_specs=row(d),
        compiler_params=_params("parallel"),
        name="outproj",
    )(h, attn, y, conv, wglu, bglu, wpw, bpw, wo)


def kernel(x, meta, ffn1_norm, ffn1_w_gate, ffn1_w_up, ffn1_w_down, mix_norm, w_in, q_norm, k_norm, ssm_lambda_re, ssm_lambda_im, ssm_log_step, ssm_b_re, ssm_b_im, ssm_c_re, ssm_c_im, ssm_d, ssm_w_glu, ssm_b_glu, conv_w_dw, conv_b_dw, conv_ln_g, conv_ln_b, conv_w_pw, conv_b_pw, w_out, ffn2_norm, ffn2_w_gate, ffn2_w_up, ffn2_w_down, post_norm):
    batch, seq, d = x.shape
    depth = ffn1_norm.shape[0]
    n_meta = meta.shape[0]
    length = seq + n_meta
    sw = ssm_w_glu.shape[1]
    cw = conv_w_pw.shape[1]
    aw = (w_in.shape[2] - sw - 2 * cw) // 3
    heads = aw // HEAD_DIM
    assert aw % LANES == 0 and sw % LANES == 0 and LANES % SSM_GROUP == 0

    p = -(-length // ATT_BLOCK) * ATT_BLOCK
    n = batch * p

    span = MXU_WIDTH if aw % MXU_WIDTH == 0 else LANES
    head_sum = jnp.kron(jnp.eye(span // HEAD_DIM, dtype=_F32),
                        jnp.ones((HEAD_DIM, HEAD_DIM), _F32)).astype(_BF16)
    ids = jnp.arange(ATT_BLOCK)
    tri = jnp.concatenate([(ids[:, None] > ids[None, :]).astype(_BF16),
                           jnp.ones((ATT_BLOCK, ATT_BLOCK), _BF16)], axis=1)
    tri = jnp.concatenate([tri, tri], axis=0)
    logit_scale = HEAD_DIM ** -0.5 * math.log2(math.e)
    vec = lambda a: a.astype(_F32).reshape(1, -1)

    for l in range(depth):
        first, last = l == 0, l == depth - 1
        h = _ffn((x, meta.astype(x.dtype)) if first else (h,), vec(ffn1_norm[l]),
                 _layer_bf16(ffn1_w_gate, l), _layer_bf16(ffn1_w_up, l),
                 _layer_bf16(ffn1_w_down, l),
                 source="embed" if first else "rows", p=p, n_meta=n_meta)
        h = h.reshape(n, d)

        q, k, v, u, ch = _inproj(
            h, vec(mix_norm[l]), _layer_bf16(w_in, l),
            vec(jnp.tile(q_norm[l].astype(_F32) * logit_scale, heads)),
            vec(jnp.tile(k_norm[l].astype(_F32), heads)), head_sum, aw, sw, cw)

        attn = _attention(q.reshape(batch, p, aw), k.reshape(batch, p, aw),
                          v.reshape(batch, p, aw), tri)
        mats = _ssm_matrices(ssm_lambda_re[l], ssm_lambda_im[l], ssm_log_step[l],
                             ssm_b_re[l], ssm_b_im[l], ssm_c_re[l], ssm_c_im[l], ssm_d[l])
        y = _ssm(u.reshape(batch, p, sw), mats)
        conv = _conv(ch.reshape(batch, p, cw), conv_w_dw[l].astype(_F32),
                     vec(conv_b_dw[l]), vec(conv_ln_g[l]), vec(conv_ln_b[l]))

        h = _outproj(h, attn.reshape(n, aw), y.reshape(n, sw), conv.reshape(n, cw),
                     _layer_bf16(ssm_w_glu, l), vec(ssm_b_glu[l]),
                     _layer_bf16(conv_w_pw, l), vec(conv_b_pw[l]),
                     _layer_bf16(w_out, l))

        h = _ffn((h.reshape(batch, p, d),), vec(ffn2_norm[l]),
                 _layer_bf16(ffn2_w_gate, l), _layer_bf16(ffn2_w_up, l),
                 _layer_bf16(ffn2_w_down, l), post_g=vec(post_norm[l]),
                 source="shifted" if last else "rows", p=p, seq=seq, n_meta=n_meta)

    return h
```
